```python
import jax, jax.numpy as jnp
from jax import lax
import numpy as np

D_MODEL = 4096
BATCH = 1
SEQ = 8192
DEPTH = 1
DEC_BATCH = 128
DEC_SEQ = 8
PAST_LEN = 2048
PAGE_SIZE = 128

ATT_HEAD_DIM = 128
ATT_SLOTS = 8
ATT_GROUPS = ((128, 1), (512, 4), (2048, 16))
N_GROUPS = len(ATT_GROUPS)
ATT_SUB_HEADS = N_GROUPS * ATT_SLOTS
ATT_WIDTH = ATT_SUB_HEADS * ATT_HEAD_DIM
ATT_OUT_WIDTH = ATT_SLOTS * ATT_HEAD_DIM
ATT_BLOCK = 128
ATT_SCALE = ATT_HEAD_DIM ** -0.5
HG_WIDTH = D_MODEL // 2
HG_EXPAND = 128
HG_HEADS = HG_WIDTH // HG_EXPAND
HG_HEAD_V = HG_WIDTH // HG_HEADS
HG_CHUNK = 64
HG_SCALE = HG_EXPAND ** -0.5
D_FF = 4 * D_MODEL
RMS_EPS = 1e-6
IN_SPLITS = [ATT_WIDTH, 2 * ATT_WIDTH, 3 * ATT_WIDTH, 3 * ATT_WIDTH + HG_WIDTH,
             3 * ATT_WIDTH + 2 * HG_WIDTH, 3 * ATT_WIDTH + 3 * HG_WIDTH]
IN_WIDTH = 3 * ATT_WIDTH + 4 * HG_WIDTH

kernel_name = 'hybrid_dilated_attn_hgrn2_decode_step'


def rmsnorm(x, gain):
    xf = x.astype(jnp.float32)
    xf = xf * lax.rsqrt(jnp.mean(xf * xf, axis=-1, keepdims=True) + RMS_EPS)
    return (xf * gain.astype(jnp.float32)).astype(x.dtype)


def dilated_window_prompt(q, k, v, dilation, span):
    b, s, h, e = q.shape
    n = s // dilation
    nb = -(-n // ATT_BLOCK)
    n_pad = nb * ATT_BLOCK

    def to_sub(t):
        t = t.reshape(b, n, dilation, h, e).transpose(0, 2, 1, 3, 4)
        return jnp.pad(t, ((0, 0), (0, 0), (0, n_pad - n), (0, 0), (0, 0)))

    def band(t):
        t = jnp.pad(to_sub(t), ((0, 0), (0, 0), (ATT_BLOCK, 0), (0, 0), (0, 0)))
        t = t.reshape(b, dilation, nb + 1, ATT_BLOCK, h, e)
        return jnp.concatenate([t[:, :, :-1], t[:, :, 1:]], axis=3)

    qb = to_sub(q).reshape(b, dilation, nb, ATT_BLOCK, h, e)
    kb, vb = band(k), band(v)
    scores = jnp.einsum('bdnqhe,bdnkhe->bdnhqk', qb, kb,
                        preferred_element_type=jnp.float32) * ATT_SCALE
    r = jnp.arange(ATT_BLOCK)[:, None]
    c = jnp.arange(2 * ATT_BLOCK)[None, :]
    dist = ATT_BLOCK + r - c
    key_pos = jnp.arange(nb)[:, None, None] * ATT_BLOCK - ATT_BLOCK + c[None]
    mask = (dist >= 0) & (dist <= span) & (key_pos >= 0)
    scores = jnp.where(mask[None, None, :, None], scores, -jnp.inf)
    lse = jax.nn.logsumexp(scores, axis=-1)
    p = jnp.exp(scores - lse[..., None]).astype(v.dtype)
    out = jnp.einsum('bdnhqk,bdnkhe->bdnqhe', p, vb)
    out = out.reshape(b, dilation, n_pad, h, e)[:, :, :n]
    out = out.transpose(0, 2, 1, 3, 4).reshape(b, s, h, e)
    lse = lse.transpose(0, 1, 2, 4, 3).reshape(b, dilation, n_pad, h)[:, :, :n]
    lse = lse.transpose(0, 2, 1, 3).reshape(b, s, h)
    return out, lse


def dilated_window_sample(q, k_new, v_new, k_buf, v_buf, dilation, span):
    b, t, h, e = q.shape
    l = k_buf.shape[1]
    kk = jnp.concatenate([k_buf, k_new.astype(k_buf.dtype)], axis=1)
    vv = jnp.concatenate([v_buf, v_new.astype(v_buf.dtype)], axis=1)
    idx = l + jnp.arange(t)[:, None] - dilation * jnp.arange(span + 1)[None, :]
    valid = idx >= 0
    flat = jnp.maximum(idx, 0).reshape(-1)
    kg = jnp.take(kk, flat, axis=1).reshape(b, t, span + 1, h, e)
    vg = jnp.take(vv, flat, axis=1).reshape(b, t, span + 1, h, e)
    scores = jnp.einsum('bthe,btmhe->bthm', q.astype(kk.dtype), kg,
                        preferred_element_type=jnp.float32) * ATT_SCALE
    scores = jnp.where(valid[None, :, None, :], scores, -jnp.inf)
    lse = jax.nn.logsumexp(scores, axis=-1)
    p = jnp.exp(scores - lse[..., None]).astype(vv.dtype)
    out = jnp.einsum('bthm,btmhe->bthe', p, vg).astype(q.dtype)
    return out, lse, kk[:, t:], vv[:, t:]


def hgrn2_recurrence(q, k, log_f, v, s0):
    b, t, h, dk = q.shape
    dv = v.shape[-1]
    chunk = min(HG_CHUNK, t)
    nc = -(-t // chunk)
    tp = nc * chunk

    def chunks(a):
        a = jnp.pad(a, ((0, 0), (0, tp - t), (0, 0), (0, 0)))
        return a.reshape(b, nc, chunk, h, a.shape[-1]).transpose(1, 0, 3, 2, 4)

    causal = jnp.tril(jnp.ones((chunk, chunk), dtype=bool))[:, :, None]

    def step(state, inp):
        qi, ki, gi, vi = inp
        cum = jnp.cumsum(gi, axis=2)
        decay = jnp.exp(jnp.where(causal, cum[:, :, :, None, :] - cum[:, :, None, :, :], -jnp.inf))
        scores = jnp.einsum('bhtc,bhsc,bhtsc->bhts', qi, ki, decay)
        o = (jnp.einsum('bhts,bhsv->bhtv', scores, vi)
             + jnp.einsum('bhtc,bhcv->bhtv', qi * jnp.exp(cum), state))
        last = cum[:, :, -1:, :]
        state = (state * jnp.exp(last[:, :, 0, :, None])
                 + jnp.einsum('bhsc,bhsv->bhcv', ki * jnp.exp(last - cum), vi))
        return state, o

    state, o = lax.scan(step, s0, (chunks(q), chunks(k), chunks(log_f), chunks(v)))
    o = o.transpose(1, 0, 3, 2, 4).reshape(b, tp, h, dv)[:, :t]
    return o, state


def layer_forward(x, hg_state, kv_bufs, lb, w_in, w_gate, b_gate, w_proj_att, w_proj_hg,
                  w_out, hg_norm, norm_mix_pre, norm_mix_post, norm_ffn_pre, norm_ffn_post,
                  w_up, w_down):
    b, t, _ = x.shape
    h = rmsnorm(x, norm_mix_pre)
    qa, ka, va, qg, fg, ig, gg = jnp.split(h @ w_in, IN_SPLITS, axis=-1)

    heads = (b, t, N_GROUPS, ATT_SLOTS, ATT_HEAD_DIM)
    qa, ka, va = qa.reshape(heads), ka.reshape(heads), va.reshape(heads)
    outs, lses, new_kv = [], [], []
    for g, (window, dilation) in enumerate(ATT_GROUPS):
        span = window // dilation
        if kv_bufs is None:
            o, lse = dilated_window_prompt(qa[:, :, g], ka[:, :, g], va[:, :, g], dilation, span)
            keep = min(window, t)
            new_kv += [ka[:, t - keep:, g], va[:, t - keep:, g]]
        else:
            o, lse, kn, vn = dilated_window_sample(qa[:, :, g], ka[:, :, g], va[:, :, g],
                                                   kv_bufs[2 * g], kv_bufs[2 * g + 1], dilation, span)
            new_kv += [kn, vn]
        outs.append(o)
        lses.append(lse)
    mix_w = jax.nn.softmax(jnp.stack(lses, axis=0), axis=0).astype(x.dtype)
    att = jnp.einsum('gbth,gbthe->bthe', mix_w, jnp.stack(outs, axis=0)).reshape(b, t, ATT_OUT_WIDTH)

    hq = (b, t, HG_HEADS, HG_EXPAND)
    hv = (b, t, HG_HEADS, HG_HEAD_V)
    q_h = jax.nn.silu(qg.astype(jnp.float32)).reshape(hq) * HG_SCALE
    forget = lb + (1.0 - lb) * jax.nn.sigmoid(fg.astype(jnp.float32).reshape(hq))
    o_h, new_state = hgrn2_recurrence(q_h, 1.0 - forget, jnp.log(forget),
                                      ig.astype(jnp.float32).reshape(hv), hg_state.astype(jnp.float32))
    o_h = rmsnorm(o_h, hg_norm) * jax.nn.silu(gg.astype(jnp.float32).reshape(hv))
    hg = o_h.reshape(b, t, HG_WIDTH).astype(x.dtype)

    gates = jax.nn.sigmoid((h @ w_gate + b_gate).astype(jnp.float32)).astype(x.dtype)
    merged = gates[..., :D_MODEL] * (att @ w_proj_att) + gates[..., D_MODEL:] * (hg @ w_proj_hg)
    x = x + rmsnorm(merged @ w_out, norm_mix_post)

    u = jax.nn.relu(rmsnorm(x, norm_ffn_pre) @ w_up)
    x = x + rmsnorm((u * u) @ w_down, norm_ffn_post)
    return x, new_kv, new_state.astype(hg_state.dtype)


def setup_inputs(seed: int = 0) -> dict:
    key = jax.random.key(seed)
    ks = jax.random.split(key, 24)

    def nrm(k, shape, scale=1.0):
        return scale * jax.random.normal(k, shape, jnp.float32)

    lens = [min(w, PAST_LEN) for (w, _) in ATT_GROUPS]
    cshape = lambda l: (DEPTH, DEC_BATCH, l, ATT_SLOTS, ATT_HEAD_DIM)
    return {
        'x_prompt': nrm(ks[0], (BATCH, SEQ, D_MODEL)),
        'x_sample': nrm(ks[1], (DEC_BATCH, DEC_SEQ, D_MODEL)),
        'cache_k_w128': nrm(ks[2], cshape(lens[0])),
        'cache_v_w128': nrm(ks[3], cshape(lens[0])),
        'cache_k_w512': nrm(ks[4], cshape(lens[1])),
        'cache_v_w512': nrm(ks[5], cshape(lens[1])),
        'cache_k_w2048': nrm(ks[6], cshape(lens[2])),
        'cache_v_w2048': nrm(ks[7], cshape(lens[2])),
        'state_hgrn': nrm(ks[8], (DEPTH, DEC_BATCH, HG_HEADS, HG_EXPAND, HG_HEAD_V), 0.5),
        'hg_lower_bound': nrm(ks[9], (DEPTH + 1, HG_WIDTH), 0.5),
        'w_in': nrm(ks[10], (DEPTH, D_MODEL, IN_WIDTH), D_MODEL ** -0.5),
        'w_gate': nrm(ks[11], (DEPTH, D_MODEL, 2 * D_MODEL), D_MODEL ** -0.5),
        'b_gate': nrm(ks[12], (DEPTH, 2 * D_MODEL), 0.01),
        'w_proj_att': nrm(ks[13], (DEPTH, ATT_OUT_WIDTH, D_MODEL), ATT_OUT_WIDTH ** -0.5),
        'w_proj_hg': nrm(ks[14], (DEPTH, HG_WIDTH, D_MODEL), HG_WIDTH ** -0.5),
        'w_out': nrm(ks[15], (DEPTH, D_MODEL, D_MODEL), D_MODEL ** -0.5),
        'hg_norm': 1.0 + nrm(ks[16], (DEPTH, HG_HEAD_V), 0.01),
        'norm_mix_pre': 1.0 + nrm(ks[17], (DEPTH, D_MODEL), 0.01),
        'norm_mix_post': 1.0 + nrm(ks[18], (DEPTH, D_MODEL), 0.01),
        'norm_ffn_pre': 1.0 + nrm(ks[19], (DEPTH, D_MODEL), 0.01),
        'norm_ffn_post': 1.0 + nrm(ks[20], (DEPTH, D_MODEL), 0.01),
        'w_up': nrm(ks[21], (DEPTH, D_MODEL, D_FF), D_MODEL ** -0.5),
        'w_down': nrm(ks[22], (DEPTH, D_FF, D_MODEL), D_FF ** -0.5),
    }


def reference(x_prompt, x_sample, cache_k_w128, cache_v_w128, cache_k_w512, cache_v_w512,
              cache_k_w2048, cache_v_w2048, state_hgrn, hg_lower_bound, w_in, w_gate, b_gate,
              w_proj_att, w_proj_hg, w_out, hg_norm, norm_mix_pre, norm_mix_post,
              norm_ffn_pre, norm_ffn_post, w_up, w_down):
    lb_all = jnp.cumsum(jax.nn.softmax(hg_lower_bound.astype(jnp.float32), axis=0), axis=0)
    y_prompt, y_sample = x_prompt, x_sample
    prompt_rows, sample_rows = [], []
    for l in range(DEPTH):
        lw = (w_in[l], w_gate[l], b_gate[l], w_proj_att[l], w_proj_hg[l], w_out[l], hg_norm[l],
              norm_mix_pre[l], norm_mix_post[l], norm_ffn_pre[l], norm_ffn_post[l], w_up[l], w_down[l])
        lb = lb_all[l].reshape(HG_HEADS, HG_EXPAND)
        zero_state = jnp.zeros((x_prompt.shape[0], HG_HEADS, HG_EXPAND, HG_HEAD_V), x_prompt.dtype)
        y_prompt, kv_p, st_p = layer_forward(y_prompt, zero_state, None, lb, *lw)
        bufs = (cache_k_w128[l], cache_v_w128[l], cache_k_w512[l], cache_v_w512[l],
                cache_k_w2048[l], cache_v_w2048[l])
        y_sample, kv_s, st_s = layer_forward(y_sample, state_hgrn[l], bufs, lb, *lw)
        prompt_rows.append(kv_p + [st_p])
        sample_rows.append(kv_s + [st_s])
    p = [jnp.stack([rows[j] for rows in prompt_rows], axis=0) for j in range(7)]
    s = [jnp.stack([rows[j] for rows in sample_rows], axis=0) for j in range(7)]
    return (y_prompt, y_sample, p[0], p[1], p[2], p[3], p[4], p[5], p[6],
            s[0], s[1], s[2], s[3], s[4], s[5], s[6])
```

```python
import functools

import jax
import jax.numpy as jnp
from jax import lax
from jax.experimental import pallas as pl
from jax.experimental.pallas import tpu as pltpu

ATT_HEAD_DIM = 128
ATT_SLOTS = 8
ATT_GROUPS = ((128, 1), (512, 4), (2048, 16))
ATT_BLOCK = 128
ATT_OUT_WIDTH = ATT_SLOTS * ATT_HEAD_DIM
ATT_WIDTH = len(ATT_GROUPS) * ATT_OUT_WIDTH
ATT_SCALE = ATT_HEAD_DIM ** -0.5
HG_EXPAND = 128
HG_HEAD_V = 128
HG_SCALE = HG_EXPAND ** -0.5
HG_CHUNK = 64
HG_SUB = 8
RMS_EPS = 1e-6
NEG_BIG = -1e30

V7X_VMEM_BYTES = 64 * 1024 * 1024
VMEM_LIMIT = V7X_VMEM_BYTES - 8 * 1024 * 1024
LANES = 128
SUBLANES = 8

BF16 = jnp.bfloat16
F32 = jnp.float32


def _params(*sem):
    return pltpu.CompilerParams(dimension_semantics=sem, vmem_limit_bytes=VMEM_LIMIT)


def _tile(n, target, mult):
    if n <= target:
        return n
    t = (target // mult) * mult
    while t >= mult:
        if n % t == 0:
            return t
        t -= mult
    raise ValueError(f"no tile for {n}")


def _sigmoid(x):
    return 1.0 / (1.0 + jnp.exp(-x))


def _dot(a, b):
    return jnp.dot(a, b, preferred_element_type=F32)


def _dot_nt(a, b):
    return lax.dot_general(a, b, (((1,), (1,)), ((), ())), preferred_element_type=F32)


def _dot_tn(a, b):
    return lax.dot_general(a, b, (((0,), (0,)), ((), ())), preferred_element_type=F32)


def _rms(x, gain):
    return x * lax.rsqrt(jnp.mean(x * x, axis=-1, keepdims=True) + RMS_EPS) * gain


def _norm_cast_kernel(x_ref, g_ref, o_ref):
    o_ref[...] = _rms(x_ref[...], g_ref[...]).astype(o_ref.dtype)


def _norm_cast(x, gain):
    m, d = x.shape
    tm = _tile(m, 256, SUBLANES)
    return pl.pallas_call(
        _norm_cast_kernel,
        grid=(m // tm,),
        in_specs=[pl.BlockSpec((tm, d), lambda i: (i, 0)), pl.BlockSpec((1, d), lambda i: (0, 0))],
        out_specs=pl.BlockSpec((tm, d), lambda i: (i, 0)),
        out_shape=jax.ShapeDtypeStruct((m, d), BF16),
        compiler_params=_params("parallel"),
        name="norm_cast",
    )(x, gain.reshape(1, d))


def _resid_norm_kernel(x_ref, y_ref, gpost_ref, gpre_ref, x1_ref, h_ref):
    x1 = x_ref[...] + _rms(y_ref[...], gpost_ref[...])
    x1_ref[...] = x1
    h_ref[...] = _rms(x1, gpre_ref[...]).astype(h_ref.dtype)


def _resid_norm(x, y, gain_post, gain_pre):
    m, d = x.shape
    tm = _tile(m, 256, SUBLANES)
    row = pl.BlockSpec((tm, d), lambda i: (i, 0))
    vec = pl.BlockSpec((1, d), lambda i: (0, 0))
    return pl.pallas_call(
        _resid_norm_kernel,
        grid=(m // tm,),
        in_specs=[row, row, vec, vec],
        out_specs=[row, row],
        out_shape=[jax.ShapeDtypeStruct((m, d), F32), jax.ShapeDtypeStruct((m, d), BF16)],
        compiler_params=_params("parallel"),
        name="resid_norm",
    )(x, y, gain_post.reshape(1, d), gain_pre.reshape(1, d))


def _resid_final_kernel(x_ref, y_ref, g_ref, o_ref):
    o_ref[...] = x_ref[...] + _rms(y_ref[...], g_ref[...])


def _resid_final(x, y, gain):
    m, d = x.shape
    tm = _tile(m, 256, SUBLANES)
    row = pl.BlockSpec((tm, d), lambda i: (i, 0))
    vec = pl.BlockSpec((1, d), lambda i: (0, 0))
    return pl.pallas_call(
        _resid_final_kernel,
        grid=(m // tm,),
        in_specs=[row, row, vec],
        out_specs=row,
        out_shape=jax.ShapeDtypeStruct((m, d), F32),
        compiler_params=_params("parallel"),
        name="resid_final",
    )(x, y, gain.reshape(1, d))


def _mm_kernel(a_ref, w_ref, o_ref, *, act):
    acc = _dot(a_ref[...], w_ref[...])
    if act == "relu2":
        acc = jnp.maximum(acc, 0.0)
        acc = acc * acc
    o_ref[...] = acc.astype(o_ref.dtype)


def _mm_bias_sigmoid_kernel(a_ref, w_ref, b_ref, o_ref):
    o_ref[...] = _sigmoid(_dot(a_ref[...], w_ref[...]) + b_ref[...]).astype(o_ref.dtype)


def _matmul(a, w, *, out_dtype=F32, act=None, bias=None, name="matmul"):
    m, k = a.shape
    n = w.shape[1]
    tm = _tile(m, 1024, SUBLANES)
    tn = _tile(n, 1024, LANES)
    a_spec = pl.BlockSpec((tm, k), lambda i, j: (i, 0))
    w_spec = pl.BlockSpec((k, tn), lambda i, j: (0, j))
    o_spec = pl.BlockSpec((tm, tn), lambda i, j: (i, j))
    if bias is None:
        body, ins, specs = functools.partial(_mm_kernel, act=act), (a, w), [a_spec, w_spec]
    else:
        body, ins = _mm_bias_sigmoid_kernel, (a, w, bias.reshape(1, n))
        specs = [a_spec, w_spec, pl.BlockSpec((1, tn), lambda i, j: (0, j))]
    return pl.pallas_call(
        body,
        grid=(m // tm, n // tn),
        in_specs=specs,
        out_specs=o_spec,
        out_shape=jax.ShapeDtypeStruct((m, n), out_dtype),
        compiler_params=_params("parallel", "arbitrary"),
        name=name,
    )(*ins)


def _mm_acc_kernel(a_ref, w_ref, o_ref):
    @pl.when(pl.program_id(2) == 0)
    def _():
        o_ref[...] = jnp.zeros_like(o_ref)

    o_ref[...] += _dot(a_ref[...], w_ref[...])


def _matmul_ksplit(a, w, *, tk, name):
    m, k = a.shape
    n = w.shape[1]
    tm = _tile(m, 1024, SUBLANES)
    tn = _tile(n, 1024, LANES)
    tk = _tile(k, tk, LANES)
    return pl.pallas_call(
        _mm_acc_kernel,
        grid=(m // tm, n // tn, k // tk),
        in_specs=[pl.BlockSpec((tm, tk), lambda i, j, l: (i, l)),
                  pl.BlockSpec((tk, tn), lambda i, j, l: (l, j))],
        out_specs=pl.BlockSpec((tm, tn), lambda i, j, l: (i, j)),
        out_shape=jax.ShapeDtypeStruct((m, n), F32),
        compiler_params=_params("parallel", "arbitrary", "arbitrary"),
        name=name,
    )(a, w)


def _merge_kernel(att_ref, hg_ref, wa_ref, wh_ref, ga_ref, gh_ref, o_ref):
    pa = _dot(att_ref[...], wa_ref[...])
    ph = _dot(hg_ref[...], wh_ref[...])
    o_ref[...] = (ga_ref[...] * pa + gh_ref[...] * ph).astype(o_ref.dtype)


def _merge_proj(att, hg, w_att, w_hg, gates):
    m, ka = att.shape
    kh = hg.shape[1]
    d = w_att.shape[1]
    tm = _tile(m, 1024, SUBLANES)
    tn = _tile(d, 1024, LANES)
    nj = d // tn
    return pl.pallas_call(
        _merge_kernel,
        grid=(m // tm, nj),
        in_specs=[pl.BlockSpec((tm, ka), lambda i, j: (i, 0)),
                  pl.BlockSpec((tm, kh), lambda i, j: (i, 0)),
                  pl.BlockSpec((ka, tn), lambda i, j: (0, j)),
                  pl.BlockSpec((kh, tn), lambda i, j: (0, j)),
                  pl.BlockSpec((tm, tn), lambda i, j: (i, j)),
                  pl.BlockSpec((tm, tn), lambda i, j: (i, j + nj))],
        out_specs=pl.BlockSpec((tm, tn), lambda i, j: (i, j)),
        out_shape=jax.ShapeDtypeStruct((m, d), BF16),
        compiler_params=_params("parallel", "arbitrary"),
        name="merge_proj",
    )(att, hg, w_att, w_hg, gates, gates)


def _attn_prompt_kernel(q_ref, kp_ref, kc_ref, vp_ref, vc_ref, o_ref, lse_ref):
    jb = pl.program_id(1)
    blk = ATT_BLOCK
    r = lax.broadcasted_iota(jnp.int32, (blk, 2 * blk), 0)
    c = lax.broadcasted_iota(jnp.int32, (blk, 2 * blk), 1)
    dist = blk + r - c
    mask = (dist >= 0) & (dist <= blk) & ((c >= blk) | (jb > 0))
    for h in range(ATT_SLOTS):
        sl = slice(h * ATT_HEAD_DIM, (h + 1) * ATT_HEAD_DIM)
        q = q_ref[:, sl].astype(BF16)
        k = jnp.concatenate([kp_ref[:, sl], kc_ref[:, sl]], axis=0).astype(BF16)
        v = jnp.concatenate([vp_ref[:, sl], vc_ref[:, sl]], axis=0).astype(BF16)
        s = jnp.where(mask, _dot_nt(q, k) * ATT_SCALE, NEG_BIG)
        m = jnp.max(s, axis=-1, keepdims=True)
        p = jnp.exp(s - m)
        l = jnp.sum(p, axis=-1, keepdims=True)
        o_ref[:, sl] = _dot(p.astype(BF16), v) / l
        lse_ref[:, sl] = jnp.broadcast_to(m + jnp.log(l), (blk, ATT_HEAD_DIM))


def _attn_prompt(proj, seq, group, dilation):
    rows, width = proj.shape
    cols = width // ATT_OUT_WIDTH
    n = seq // dilation
    nb = n // ATT_BLOCK
    view = proj.reshape(rows // dilation, dilation * width)
    blk = (ATT_BLOCK, ATT_OUT_WIDTH)
    ngroups = len(ATT_GROUPS)

    def spec(col, prev):
        if prev:
            return pl.BlockSpec(blk, lambda r, j: (jnp.maximum(j - 1, 0), r * cols + col))
        return pl.BlockSpec(blk, lambda r, j: (j, r * cols + col))

    out_spec = pl.BlockSpec(blk, lambda r, j: (j, r))
    shape = jax.ShapeDtypeStruct((n, dilation * ATT_OUT_WIDTH), F32)
    out, lse = pl.pallas_call(
        _attn_prompt_kernel,
        grid=(dilation, nb),
        in_specs=[spec(group, False),
                  spec(ngroups + group, True), spec(ngroups + group, False),
                  spec(2 * ngroups + group, True), spec(2 * ngroups + group, False)],
        out_specs=[out_spec, out_spec],
        out_shape=[shape, shape],
        compiler_params=_params("parallel", "arbitrary"),
        name=f"attn_prompt_d{dilation}",
    )(view, view, view, view, view)
    return out.reshape(seq, ATT_OUT_WIDTH), lse.reshape(seq, ATT_OUT_WIDTH)


def _attn_sample_kernel(q_ref, kn_ref, vn_ref, kb_ref, vb_ref, kx_ref, vx_ref,
                        o_ref, lse_ref, ko_ref, vo_ref, m_sc, l_sc, acc_sc, *, dilation, t_new):
    lt = pl.program_id(1)
    nlt = pl.num_programs(1)
    tl = kb_ref.shape[1]
    rows = ATT_SLOTS * t_new
    dmask = dilation - 1

    q = q_ref[0]
    qt = jnp.concatenate([q] * ATT_SLOTS, axis=0)
    hrow = lax.broadcasted_iota(jnp.int32, (rows, ATT_OUT_WIDTH), 0) // t_new
    hcol = lax.broadcasted_iota(jnp.int32, (rows, ATT_OUT_WIDTH), 1) // ATT_HEAD_DIM
    qbd = jnp.where(hrow == hcol, qt, 0.0).astype(BF16)

    @pl.when(lt == 0)
    def _():
        s = _dot_nt(qbd, kn_ref[0].astype(BF16)) * ATT_SCALE
        t = lax.broadcasted_iota(jnp.int32, s.shape, 0) % t_new
        i = lax.broadcasted_iota(jnp.int32, s.shape, 1)
        s = jnp.where((i <= t) & (((t - i) & dmask) == 0), s, NEG_BIG)
        m = jnp.max(s, axis=-1, keepdims=True)
        p = jnp.exp(s - m)
        m_sc[...] = m
        l_sc[...] = jnp.sum(p, axis=-1, keepdims=True)
        acc_sc[...] = _dot(p.astype(BF16), vn_ref[0].astype(BF16))

    s = _dot_nt(qbd, kb_ref[0].astype(BF16)) * ATT_SCALE
    t = lax.broadcasted_iota(jnp.int32, s.shape, 0) % t_new
    j = lax.broadcasted_iota(jnp.int32, s.shape, 1) + lt * tl
    s = jnp.where((j >= t) & (((j - t) & dmask) == 0), s, NEG_BIG)
    m_old = m_sc[...]
    m_new = jnp.maximum(m_old, jnp.max(s, axis=-1, keepdims=True))
    alpha = jnp.exp(m_old - m_new)
    p = jnp.exp(s - m_new)
    l_sc[...] = alpha * l_sc[...] + jnp.sum(p, axis=-1, keepdims=True)
    acc_sc[...] = alpha * acc_sc[...] + _dot(p.astype(BF16), vb_ref[0].astype(BF16))
    m_sc[...] = m_new

    ko_ref[0, :tl - t_new] = kb_ref[0, t_new:]
    vo_ref[0, :tl - t_new] = vb_ref[0, t_new:]

    @pl.when(lt < nlt - 1)
    def _():
        ko_ref[0, tl - t_new:] = kx_ref[0]
        vo_ref[0, tl - t_new:] = vx_ref[0]

    @pl.when(lt == nlt - 1)
    def _():
        ko_ref[0, tl - t_new:] = kn_ref[0]
        vo_ref[0, tl - t_new:] = vn_ref[0]
        l = l_sc[...]
        out = acc_sc[...] / l
        lse = m_sc[...] + jnp.log(l)
        for h in range(ATT_SLOTS):
            sl = slice(h * ATT_HEAD_DIM, (h + 1) * ATT_HEAD_DIM)
            rs = slice(h * t_new, (h + 1) * t_new)
            o_ref[0, :, sl] = out[rs, sl]
            lse_ref[0, :, sl] = jnp.broadcast_to(lse[rs], (t_new, ATT_HEAD_DIM))


def _attn_sample(proj3, batch0, group, dilation, k_buf, v_buf):
    db, length, _ = k_buf.shape
    t_new = proj3.shape[1]
    ngroups = len(ATT_GROUPS)
    tl = _tile(length, 1024, SUBLANES)
    nlt = length // tl
    step = tl // t_new
    last = length // t_new - 1

    def new_spec(col):
        return pl.BlockSpec((1, t_new, ATT_OUT_WIDTH), lambda b, l: (batch0 + b, 0, col))

    buf_spec = pl.BlockSpec((1, tl, ATT_OUT_WIDTH), lambda b, l: (b, l, 0))
    next_spec = pl.BlockSpec((1, t_new, ATT_OUT_WIDTH),
                             lambda b, l: (b, jnp.minimum((l + 1) * step, last), 0))
    small = pl.BlockSpec((1, t_new, ATT_OUT_WIDTH), lambda b, l: (b, 0, 0))
    small_shape = jax.ShapeDtypeStruct((db, t_new, ATT_OUT_WIDTH), F32)
    buf_shape = jax.ShapeDtypeStruct(k_buf.shape, F32)
    rows = ATT_SLOTS * t_new
    return pl.pallas_call(
        functools.partial(_attn_sample_kernel, dilation=dilation, t_new=t_new),
        grid=(db, nlt),
        in_specs=[new_spec(group), new_spec(ngroups + group), new_spec(2 * ngroups + group),
                  buf_spec, buf_spec, next_spec, next_spec],
        out_specs=[small, small, buf_spec, buf_spec],
        out_shape=[small_shape, small_shape, buf_shape, buf_shape],
        scratch_shapes=[pltpu.VMEM((rows, 1), F32), pltpu.VMEM((rows, 1), F32),
                        pltpu.VMEM((rows, ATT_OUT_WIDTH), F32)],
        compiler_params=_params("parallel", "arbitrary"),
        name=f"attn_sample_d{dilation}",
    )(proj3, proj3, proj3, k_buf, v_buf, k_buf, v_buf)


def _combine_kernel(o0, o1, o2, l0, l1, l2, out_ref):
    a, b, c = l0[...], l1[...], l2[...]
    m = jnp.maximum(jnp.maximum(a, b), c)
    ea, eb, ec = jnp.exp(a - m), jnp.exp(b - m), jnp.exp(c - m)
    num = ea * o0[...] + eb * o1[...] + ec * o2[...]
    out_ref[...] = (num / (ea + eb + ec)).astype(out_ref.dtype)


def _combine(outs, lses):
    m, w = outs[0].shape
    tm = _tile(m, 512, SUBLANES)
    spec = pl.BlockSpec((tm, w), lambda i: (i, 0))
    return pl.pallas_call(
        _combine_kernel,
        grid=(m // tm,),
        in_specs=[spec] * 6,
        out_specs=spec,
        out_shape=jax.ShapeDtypeStruct((m, w), BF16),
        compiler_params=_params("parallel"),
        name="attn_combine",
    )(*outs, *lses)


def _cumsum_rows(x):
    c = x.shape[0]
    r = lax.broadcasted_iota(jnp.int32, (c, c), 0)
    s = lax.broadcasted_iota(jnp.int32, (c, c), 1)
    tri = jnp.where(r >= s, 1.0, 0.0).astype(BF16)
    hi = x.astype(BF16)
    rem = x - hi.astype(F32)
    mid = rem.astype(BF16)
    lo = (rem - mid.astype(F32)).astype(BF16)
    return _dot(tri, hi) + _dot(tri, mid) + _dot(tri, lo)


def _lower_bound(lb_ref):
    a = lb_ref[...]
    e = jnp.exp(a - jnp.max(a, axis=0, keepdims=True))
    return e[0:1] / jnp.sum(e, axis=0, keepdims=True)


def _hgrn_chunk(qg, fg, ig, gg, lb, gain, st):
    c = qg.shape[0]
    q = qg * _sigmoid(qg) * HG_SCALE
    forget = lb + (1.0 - lb) * _sigmoid(fg)
    k = 1.0 - forget
    v = ig
    b = _cumsum_rows(jnp.log(forget))
    v16 = v.astype(BF16)

    nsub = c // HG_SUB
    b3 = b.reshape(nsub, HG_SUB, LANES)
    q3 = q.reshape(nsub, HG_SUB, LANES)
    k3 = k.reshape(nsub, HG_SUB, LANES)
    pos = lax.broadcasted_iota(jnp.int32, (nsub, HG_SUB, LANES), 1)
    row = lax.broadcasted_iota(jnp.int32, (c, c), 0)
    col = lax.broadcasted_iota(jnp.int32, (c, c), 1)
    scores = jnp.zeros((c, c), F32)
    for s in range(HG_SUB):
        diff = jnp.where(pos >= s, b3 - b3[:, s:s + 1, :], NEG_BIG)
        w = jnp.sum(q3 * k3[:, s:s + 1, :] * jnp.exp(diff), axis=-1, keepdims=True)
        w = jnp.broadcast_to(w.reshape(c, 1), (c, c))
        scores = jnp.where(col == (row // HG_SUB) * HG_SUB + s, w, scores)

    width = HG_SUB
    while width < c:
        pair = 2 * width
        bm = jnp.concatenate(
            [jnp.broadcast_to(b[p * pair + width - 1:p * pair + width], (pair, LANES))
             for p in range(c // pair)], axis=0)
        e = jnp.exp(-jnp.abs(b - bm))
        right = (lax.broadcasted_iota(jnp.int32, (c, LANES), 0) // width) % 2 == 1
        ql = jnp.where(right, q * e, 0.0).astype(BF16)
        kl = jnp.where(right, 0.0, k * e).astype(BF16)
        a = _dot_nt(ql, kl)
        scores = jnp.where((row // pair == col // pair) & (row // width != col // width) & (row > col),
                           a, scores)
        width = pair

    o = _dot(scores.astype(BF16), v16) + _dot_nt((q * jnp.exp(b)).astype(BF16), st.astype(BF16))
    b_last = b[c - 1:c]
    st_new = st * jnp.exp(b_last) + _dot_tn(v16, (k * jnp.exp(b_last - b)).astype(BF16))
    out = _rms(o, gain) * (gg * _sigmoid(gg))
    return out, st_new


def _hgrn_prompt_kernel(q_ref, f_ref, i_ref, g_ref, lb_ref, gain_ref, o_ref, s_ref, st_sc):
    tb = pl.program_id(1)

    @pl.when(tb == 0)
    def _():
        st_sc[...] = jnp.zeros_like(st_sc)

    lb = _lower_bound(lb_ref)
    gain = gain_ref[...]
    st = st_sc[...]
    for ci in range(q_ref.shape[0] // HG_CHUNK):
        rs = slice(ci * HG_CHUNK, (ci + 1) * HG_CHUNK)
        out, st = _hgrn_chunk(q_ref[rs, :], f_ref[rs, :], i_ref[rs, :], g_ref[rs, :], lb, gain, st)
        o_ref[rs, :] = out.astype(o_ref.dtype)
    st_sc[...] = st

    @pl.when(tb == pl.num_programs(1) - 1)
    def _():
        s_ref[0] = st.T


def _hgrn_prompt(proj, seq, hg_lower_bound, hg_norm, hg_width):
    heads = hg_width // HG_EXPAND
    base = 3 * ATT_WIDTH // LANES
    per = hg_width // LANES
    tb = _tile(seq, 256, HG_CHUNK)

    def col(which):
        return pl.BlockSpec((tb, LANES), lambda h, t: (t, base + which * per + h))

    depth1 = hg_lower_bound.shape[0]
    return pl.pallas_call(
        _hgrn_prompt_kernel,
        grid=(heads, seq // tb),
        in_specs=[col(0), col(1), col(2), col(3),
                  pl.BlockSpec((depth1, LANES), lambda h, t: (0, h)),
                  pl.BlockSpec((1, LANES), lambda h, t: (0, 0))],
        out_specs=[pl.BlockSpec((tb, LANES), lambda h, t: (t, h)),
                   pl.BlockSpec((1, HG_EXPAND, HG_HEAD_V), lambda h, t: (h, 0, 0))],
        out_shape=[jax.ShapeDtypeStruct((seq, hg_width), BF16),
                   jax.ShapeDtypeStruct((heads, HG_EXPAND, HG_HEAD_V), F32)],
        scratch_shapes=[pltpu.VMEM((HG_HEAD_V, HG_EXPAND), F32)],
        compiler_params=_params("parallel", "arbitrary"),
        name="hgrn_prompt",
    )(proj, proj, proj, proj, hg_lower_bound, hg_norm.reshape(1, LANES))


def _hgrn_sample_kernel(q_ref, f_ref, i_ref, g_ref, lb_ref, gain_ref, s_ref, o_ref, so_ref):
    gain = gain_ref[...]
    for h in range(s_ref.shape[1]):
        sl = slice(h * LANES, (h + 1) * LANES)
        lb = _lower_bound(lb_ref.at[:, sl])
        out, st = _hgrn_chunk(q_ref[0, :, sl], f_ref[0, :, sl], i_ref[0, :, sl], g_ref[0, :, sl],
                              lb, gain, s_ref[0, h].T)
        o_ref[0, :, sl] = out.astype(o_ref.dtype)
        so_ref[0, h] = st.T


def _hgrn_sample(proj3, batch0, state, hg_lower_bound, hg_norm):
    db, heads = state.shape[:2]
    t_new = proj3.shape[1]
    hg_width = heads * HG_EXPAND
    cw = _tile(hg_width, ATT_OUT_WIDTH, LANES)
    assert 3 * ATT_WIDTH % cw == 0
    base = 3 * ATT_WIDTH // cw
    per = hg_width // cw
    hb = cw // HG_EXPAND

    def col(which):
        return pl.BlockSpec((1, t_new, cw), lambda b, c: (batch0 + b, 0, base + which * per + c))

    depth1 = hg_lower_bound.shape[0]
    st_spec = pl.BlockSpec((1, hb, HG_EXPAND, HG_HEAD_V), lambda b, c: (b, c, 0, 0))
    return pl.pallas_call(
        _hgrn_sample_kernel,
        grid=(db, per),
        in_specs=[col(0), col(1), col(2), col(3),
                  pl.BlockSpec((depth1, cw), lambda b, c: (0, c)),
                  pl.BlockSpec((1, LANES), lambda b, c: (0, 0)),
                  st_spec],
        out_specs=[pl.BlockSpec((1, t_new, cw), lambda b, c: (b, 0, c)), st_spec],
        out_shape=[jax.ShapeDtypeStruct((db, t_new, hg_width), BF16),
                   jax.ShapeDtypeStruct(state.shape, F32)],
        compiler_params=_params("parallel", "parallel"),
        name="hgrn_sample",
    )(proj3, proj3, proj3, proj3, hg_lower_bound, hg_norm.reshape(1, LANES), state)


def kernel(x_prompt, x_sample, cache_k_w128, cache_v_w128, cache_k_w512, cache_v_w512,
           cache_k_w2048, cache_v_w2048, state_hgrn, hg_lower_bound, w_in, w_gate, b_gate,
           w_proj_att, w_proj_hg, w_out, hg_norm, norm_mix_pre, norm_mix_post,
           norm_ffn_pre, norm_ffn_post, w_up, w_down):
    assert w_in.shape[0] == 1, "single-layer trunk"
    batch, seq, d_model = x_prompt.shape
    db, t_new, _ = x_sample.shape
    assert batch == 1
    hg_width = w_proj_hg.shape[1]
    n_prompt = batch * seq
    n_sample = db * t_new
    caches = (cache_k_w128, cache_v_w128, cache_k_w512, cache_v_w512, cache_k_w2048, cache_v_w2048)

    x = jnp.concatenate([x_prompt.reshape(n_prompt, d_model), x_sample.reshape(n_sample, d_model)], axis=0)
    h = _norm_cast(x, norm_mix_pre[0])
    proj = _matmul(h, w_in[0].astype(BF16), name="in_proj")
    gates = _matmul(h, w_gate[0].astype(BF16), bias=b_gate[0], name="gate_proj")
    in_width = proj.shape[1]
    proj3 = proj.reshape((n_prompt + n_sample) // t_new, t_new, in_width)
    batch0 = n_prompt // t_new

    outs_p, lses_p, outs_s, lses_s, new_kv_p, new_kv_s = [], [], [], [], [], []
    for g, (window, dilation) in enumerate(ATT_GROUPS):
        o, lse = _attn_prompt(proj, seq, g, dilation)
        outs_p.append(o)
        lses_p.append(lse)
        keep = min(window, seq)
        for part in (1, 2):
            c0 = part * ATT_WIDTH + g * ATT_OUT_WIDTH
            rows = proj[n_prompt - keep:n_prompt, c0:c0 + ATT_OUT_WIDTH]
            new_kv_p.append(rows.reshape(1, batch, keep, ATT_SLOTS, ATT_HEAD_DIM))
        k_buf, v_buf = caches[2 * g][0], caches[2 * g + 1][0]
        length = k_buf.shape[1]
        assert length == window and length == dilation * ATT_BLOCK
        o, lse, k_new, v_new = _attn_sample(proj3, batch0, g, dilation,
                                            k_buf.reshape(db, length, ATT_OUT_WIDTH),
                                            v_buf.reshape(db, length, ATT_OUT_WIDTH))
        outs_s.append(o.reshape(n_sample, ATT_OUT_WIDTH))
        lses_s.append(lse.reshape(n_sample, ATT_OUT_WIDTH))
        new_kv_s += [k_new.reshape(1, db, length, ATT_SLOTS, ATT_HEAD_DIM),
                     v_new.reshape(1, db, length, ATT_SLOTS, ATT_HEAD_DIM)]
    att = _combine([jnp.concatenate([p, s], axis=0) for p, s in zip(outs_p, outs_s)],
                   [jnp.concatenate([p, s], axis=0) for p, s in zip(lses_p, lses_s)])

    hg_p, state_p = _hgrn_prompt(proj, seq, hg_lower_bound, hg_norm[0], hg_width)
    hg_s, state_s = _hgrn_sample(proj3, batch0, state_hgrn[0], hg_lower_bound, hg_norm[0])
    hg = jnp.concatenate([hg_p, hg_s.reshape(n_sample, hg_width)], axis=0)

    merged = _merge_proj(att, hg, w_proj_att[0].astype(BF16), w_proj_hg[0].astype(BF16), gates)
    mixed = _matmul(merged, w_out[0].astype(BF16), name="out_proj")
    x1, h2 = _resid_norm(x, mixed, norm_mix_post[0], norm_ffn_pre[0])

    u = _matmul(h2, w_up[0].astype(BF16), out_dtype=BF16, act="relu2", name="ffn_up")
    z = _matmul_ksplit(u, w_down[0].astype(BF16), tk=4096, name="ffn_down")
    y = _resid_final(x1, z, norm_ffn_post[0])

    y_prompt = y[:n_prompt].reshape(batch, seq, d_model)
    y_sample = y[n_prompt:].reshape(db, t_new, d_model)
    heads = hg_width // HG_EXPAND
    return (y_prompt, y_sample, *new_kv_p,
            state_p.reshape(1, batch, heads, HG_EXPAND, HG_HEAD_V),
            *new_kv_s, state_s[None])
```

```python
import functools
import math

import jax
import jax.numpy as jnp
from jax import lax
from jax.experimental import pallas as pl
from jax.experimental.pallas import tpu as pltpu

ATT_HEAD_DIM = 128
ATT_SLOTS = 8
ATT_GROUPS = ((128, 1), (512, 4), (2048, 16))
ATT_BLOCK = 128
ATT_OUT_WIDTH = ATT_SLOTS * ATT_HEAD_DIM
ATT_WIDTH = len(ATT_GROUPS) * ATT_OUT_WIDTH
ATT_SCALE = ATT_HEAD_DIM ** -0.5
HG_EXPAND = 128
HG_HEAD_V = 128
HG_SCALE = HG_EXPAND ** -0.5
HG_CHUNK = 64
HG_SUB = 8
RMS_EPS = 1e-6
NEG_BIG = -1e30

V7X_VMEM_BYTES = 64 * 1024 * 1024
VMEM_LIMIT = V7X_VMEM_BYTES - 8 * 1024 * 1024
LANES = 128
SUBLANES = 8

BF16 = jnp.bfloat16
F32 = jnp.float32


def _params(*sem):
    return pltpu.CompilerParams(dimension_semantics=sem, vmem_limit_bytes=VMEM_LIMIT)


def _tile(n, target, mult):
    if n <= target:
        return n
    t = (target // mult) * mult
    while t >= mult:
        if n % t == 0:
            return t
        t -= mult
    raise ValueError(f"no tile for {n}")


def _sigmoid(x):
    return 1.0 / (1.0 + jnp.exp(-x))


def _dot(a, b):
    return jnp.dot(a, b, preferred_element_type=F32)


def _dot_nt(a, b):
    return lax.dot_general(a, b, (((1,), (1,)), ((), ())), preferred_element_type=F32)


def _dot_tn(a, b):
    return lax.dot_general(a, b, (((0,), (0,)), ((), ())), preferred_element_type=F32)


def _rms(x, gain):
    return x * lax.rsqrt(jnp.mean(x * x, axis=-1, keepdims=True) + RMS_EPS) * gain


def _split_rows(n_prompt, n_sample, d, target):
    tm = _tile(math.gcd(n_prompt, n_sample), target, SUBLANES)
    np_tiles = n_prompt // tm
    prompt = pl.BlockSpec((tm, d), lambda i: (jnp.minimum(i, np_tiles - 1), 0))
    sample = pl.BlockSpec((tm, d), lambda i: (jnp.maximum(i - np_tiles, 0), 0))
    stacked = pl.BlockSpec((tm, d), lambda i: (i, 0))
    vec = pl.BlockSpec((1, d), lambda i: (0, 0))
    return tm, np_tiles, prompt, sample, stacked, vec


def _norm_cast_kernel(xp_ref, xs_ref, g_ref, o_ref, *, np_tiles):
    i = pl.program_id(0)

    @pl.when(i < np_tiles)
    def _():
        o_ref[...] = _rms(xp_ref[...], g_ref[...]).astype(o_ref.dtype)

    @pl.when(i >= np_tiles)
    def _():
        o_ref[...] = _rms(xs_ref[...], g_ref[...]).astype(o_ref.dtype)


def _norm_cast(xp, xs, gain):
    d = xp.shape[1]
    m = xp.shape[0] + xs.shape[0]
    tm, np_tiles, prompt, sample, stacked, vec = _split_rows(xp.shape[0], xs.shape[0], d, 256)
    return pl.pallas_call(
        functools.partial(_norm_cast_kernel, np_tiles=np_tiles),
        grid=(m // tm,),
        in_specs=[prompt, sample, vec],
        out_specs=stacked,
        out_shape=jax.ShapeDtypeStruct((m, d), BF16),
        compiler_params=_params("arbitrary"),
        name="norm_cast",
    )(xp, xs, gain.reshape(1, d))


def _resid_norm_kernel(xp_ref, xs_ref, y_ref, gpost_ref, gpre_ref, x1_ref, h_ref, *, np_tiles):
    i = pl.program_id(0)

    def body(x_ref):
        x1 = x_ref[...] + _rms(y_ref[...], gpost_ref[...])
        x1_ref[...] = x1
        h_ref[...] = _rms(x1, gpre_ref[...]).astype(h_ref.dtype)

    pl.when(i < np_tiles)(lambda: body(xp_ref))
    pl.when(i >= np_tiles)(lambda: body(xs_ref))


def _resid_norm(xp, xs, y, gain_post, gain_pre):
    m, d = y.shape
    tm, np_tiles, prompt, sample, stacked, vec = _split_rows(xp.shape[0], xs.shape[0], d, 256)
    return pl.pallas_call(
        functools.partial(_resid_norm_kernel, np_tiles=np_tiles),
        grid=(m // tm,),
        in_specs=[prompt, sample, stacked, vec, vec],
        out_specs=[stacked, stacked],
        out_shape=[jax.ShapeDtypeStruct((m, d), F32), jax.ShapeDtypeStruct((m, d), BF16)],
        compiler_params=_params("arbitrary"),
        name="resid_norm",
    )(xp, xs, y, gain_post.reshape(1, d), gain_pre.reshape(1, d))


def _resid_final_kernel(x_ref, y_ref, g_ref, op_ref, os_ref, *, np_tiles):
    i = pl.program_id(0)

    @pl.when(i < np_tiles)
    def _():
        op_ref[...] = x_ref[...] + _rms(y_ref[...], g_ref[...])

    @pl.when(i >= np_tiles)
    def _():
        os_ref[...] = x_ref[...] + _rms(y_ref[...], g_ref[...])


def _resid_final(x, y, gain, n_prompt):
    m, d = x.shape
    tm, np_tiles, prompt, sample, stacked, vec = _split_rows(n_prompt, m - n_prompt, d, 256)
    return pl.pallas_call(
        functools.partial(_resid_final_kernel, np_tiles=np_tiles),
        grid=(m // tm,),
        in_specs=[stacked, stacked, vec],
        out_specs=[prompt, sample],
        out_shape=[jax.ShapeDtypeStruct((n_prompt, d), F32),
                   jax.ShapeDtypeStruct((m - n_prompt, d), F32)],
        compiler_params=_params("arbitrary"),
        name="resid_final",
    )(x, y, gain.reshape(1, d))


def _mm_kernel(a_ref, w_ref, o_ref, *, act):
    acc = _dot(a_ref[...], w_ref[...])
    if act == "relu2":
        acc = jnp.maximum(acc, 0.0)
        acc = acc * acc
    o_ref[...] = acc.astype(o_ref.dtype)


def _mm_bias_sigmoid_kernel(a_ref, w_ref, b_ref, o_ref):
    o_ref[...] = _sigmoid(_dot(a_ref[...], w_ref[...]) + b_ref[...]).astype(o_ref.dtype)


def _matmul(a, w, *, out_dtype=F32, act=None, bias=None, name="matmul"):
    m, k = a.shape
    n = w.shape[1]
    tm = _tile(m, 1024, SUBLANES)
    tn = _tile(n, 1024, LANES)
    a_spec = pl.BlockSpec((tm, k), lambda i, j: (i, 0))
    w_spec = pl.BlockSpec((k, tn), lambda i, j: (0, j))
    o_spec = pl.BlockSpec((tm, tn), lambda i, j: (i, j))
    if bias is None:
        body, ins, specs = functools.partial(_mm_kernel, act=act), (a, w), [a_spec, w_spec]
    else:
        body, ins = _mm_bias_sigmoid_kernel, (a, w, bias.reshape(1, n))
        specs = [a_spec, w_spec, pl.BlockSpec((1, tn), lambda i, j: (0, j))]
    return pl.pallas_call(
        body,
        grid=(m // tm, n // tn),
        in_specs=specs,
        out_specs=o_spec,
        out_shape=jax.ShapeDtypeStruct((m, n), out_dtype),
        compiler_params=_params("parallel", "arbitrary"),
        name=name,
    )(*ins)


def _mm_acc_kernel(a_ref, w_ref, o_ref):
    @pl.when(pl.program_id(2) == 0)
    def _():
        o_ref[...] = jnp.zeros_like(o_ref)

    o_ref[...] += _dot(a_ref[...], w_ref[...])


def _matmul_ksplit(a, w, *, tk, name):
    m, k = a.shape
    n = w.shape[1]
    tm = _tile(m, 1024, SUBLANES)
    tn = _tile(n, 1024, LANES)
    tk = _tile(k, tk, LANES)
    return pl.pallas_call(
        _mm_acc_kernel,
        grid=(m // tm, n // tn, k // tk),
        in_specs=[pl.BlockSpec((tm, tk), lambda i, j, l: (i, l)),
                  pl.BlockSpec((tk, tn), lambda i, j, l: (l, j))],
        out_specs=pl.BlockSpec((tm, tn), lambda i, j, l: (i, j)),
        out_shape=jax.ShapeDtypeStruct((m, n), F32),
        compiler_params=_params("parallel", "arbitrary", "arbitrary"),
        name=name,
    )(a, w)


def _merge_kernel(att_ref, hg_ref, wa_ref, wh_ref, ga_ref, gh_ref, o_ref):
    pa = _dot(att_ref[...], wa_ref[...])
    ph = _dot(hg_ref[...], wh_ref[...])
    o_ref[...] = (ga_ref[...] * pa + gh_ref[...] * ph).astype(o_ref.dtype)


def _merge_proj(att, hg, w_att, w_hg, gates):
    m, ka = att.shape
    kh = hg.shape[1]
    d = w_att.shape[1]
    tm = _tile(m, 1024, SUBLANES)
    tn = _tile(d, 1024, LANES)
    nj = d // tn
    return pl.pallas_call(
        _merge_kernel,
        grid=(m // tm, nj),
        in_specs=[pl.BlockSpec((tm, ka), lambda i, j: (i, 0)),
                  pl.BlockSpec((tm, kh), lambda i, j: (i, 0)),
                  pl.BlockSpec((ka, tn), lambda i, j: (0, j)),
                  pl.BlockSpec((kh, tn), lambda i, j: (0, j)),
                  pl.BlockSpec((tm, tn), lambda i, j: (i, j)),
                  pl.BlockSpec((tm, tn), lambda i, j: (i, j + nj))],
        out_specs=pl.BlockSpec((tm, tn), lambda i, j: (i, j)),
        out_shape=jax.ShapeDtypeStruct((m, d), BF16),
        compiler_params=_params("parallel", "arbitrary"),
        name="merge_proj",
    )(att, hg, w_att, w_hg, gates, gates)


def _attn_prompt_kernel(q_ref, kp_ref, kc_ref, vp_ref, vc_ref, o_ref, lse_ref):
    jb = pl.program_id(1)
    blk = ATT_BLOCK
    r = lax.broadcasted_iota(jnp.int32, (blk, 2 * blk), 0)
    c = lax.broadcasted_iota(jnp.int32, (blk, 2 * blk), 1)
    dist = blk + r - c
    mask = (dist >= 0) & (dist <= blk) & ((c >= blk) | (jb > 0))
    for h in range(ATT_SLOTS):
        sl = slice(h * ATT_HEAD_DIM, (h + 1) * ATT_HEAD_DIM)
        q = q_ref[:, sl].astype(BF16)
        k = jnp.concatenate([kp_ref[:, sl], kc_ref[:, sl]], axis=0).astype(BF16)
        v = jnp.concatenate([vp_ref[:, sl], vc_ref[:, sl]], axis=0).astype(BF16)
        s = jnp.where(mask, _dot_nt(q, k) * ATT_SCALE, NEG_BIG)
        m = jnp.max(s, axis=-1, keepdims=True)
        p = jnp.exp(s - m)
        l = jnp.sum(p, axis=-1, keepdims=True)
        o_ref[:, sl] = _dot(p.astype(BF16), v) / l
        lse_ref[:, sl] = jnp.broadcast_to(m + jnp.log(l), (blk, ATT_HEAD_DIM))


def _attn_prompt(proj, seq, group, dilation):
    rows, width = proj.shape
    cols = width // ATT_OUT_WIDTH
    n = seq // dilation
    nb = n // ATT_BLOCK
    view = proj.reshape(rows // dilation, dilation * width)
    blk = (ATT_BLOCK, ATT_OUT_WIDTH)
    ngroups = len(ATT_GROUPS)

    def spec(col, prev):
        if prev:
            return pl.BlockSpec(blk, lambda r, j: (jnp.maximum(j - 1, 0), r * cols + col))
        return pl.BlockSpec(blk, lambda r, j: (j, r * cols + col))

    out_spec = pl.BlockSpec(blk, lambda r, j: (j, r))
    shape = jax.ShapeDtypeStruct((n, dilation * ATT_OUT_WIDTH), F32)
    out, lse = pl.pallas_call(
        _attn_prompt_kernel,
        grid=(dilation, nb),
        in_specs=[spec(group, False),
                  spec(ngroups + group, True), spec(ngroups + group, False),
                  spec(2 * ngroups + group, True), spec(2 * ngroups + group, False)],
        out_specs=[out_spec, out_spec],
        out_shape=[shape, shape],
        compiler_params=_params("parallel", "arbitrary"),
        name=f"attn_prompt_d{dilation}",
    )(view, view, view, view, view)
    return out.reshape(seq, ATT_OUT_WIDTH), lse.reshape(seq, ATT_OUT_WIDTH)


def _attn_sample_kernel(q_ref, kn_ref, vn_ref, kn2_ref, vn2_ref, kb_ref, vb_ref, kx_ref, vx_ref,
                        o_ref, lse_ref, ko_ref, vo_ref, m_sc, l_sc, acc_sc, *, dilation, t_new):
    lt = pl.program_id(1)
    nlt = pl.num_programs(1)
    nh = ATT_SLOTS
    tl = kb_ref.shape[1] // nh
    shift = t_new * nh
    dmask = dilation - 1

    def head(h):
        return slice(h * ATT_HEAD_DIM, (h + 1) * ATT_HEAD_DIM), slice(h * t_new, (h + 1) * t_new)

    @pl.when(lt == 0)
    def _():
        t = lax.broadcasted_iota(jnp.int32, (t_new, t_new), 0)
        i = lax.broadcasted_iota(jnp.int32, (t_new, t_new), 1)
        mask = (i <= t) & (((t - i) & dmask) == 0)
        for h in range(nh):
            sl, rs = head(h)
            s = _dot_nt(q_ref[0, :, sl].astype(BF16), kn_ref[0, :, sl].astype(BF16)) * ATT_SCALE
            s = jnp.where(mask, s, NEG_BIG)
            m = jnp.max(s, axis=-1, keepdims=True)
            p = jnp.exp(s - m)
            m_sc[rs] = m
            l_sc[rs] = jnp.sum(p, axis=-1, keepdims=True)
            acc_sc[rs] = _dot(p.astype(BF16), vn_ref[0, :, sl].astype(BF16))

    t = lax.broadcasted_iota(jnp.int32, (t_new, tl), 0)
    j = lax.broadcasted_iota(jnp.int32, (t_new, tl), 1) + lt * tl
    mask = (j >= t) & (((j - t) & dmask) == 0)
    for h in range(nh):
        sl, rs = head(h)
        kh = kb_ref[0, pl.ds(h, tl, stride=nh), :].astype(BF16)
        vh = vb_ref[0, pl.ds(h, tl, stride=nh), :].astype(BF16)
        s = jnp.where(mask, _dot_nt(q_ref[0, :, sl].astype(BF16), kh) * ATT_SCALE, NEG_BIG)
        m_old = m_sc[rs]
        m_new = jnp.maximum(m_old, jnp.max(s, axis=-1, keepdims=True))
        alpha = jnp.exp(m_old - m_new)
        p = jnp.exp(s - m_new)
        l_sc[rs] = alpha * l_sc[rs] + jnp.sum(p, axis=-1, keepdims=True)
        acc_sc[rs] = alpha * acc_sc[rs] + _dot(p.astype(BF16), vh)
        m_sc[rs] = m_new

    keep = tl * nh - shift
    ko_ref[0, :keep] = kb_ref[0, shift:]
    vo_ref[0, :keep] = vb_ref[0, shift:]

    @pl.when(lt < nlt - 1)
    def _():
        ko_ref[0, keep:] = kx_ref[0]
        vo_ref[0, keep:] = vx_ref[0]

    @pl.when(lt == nlt - 1)
    def _():
        ko_ref[0, keep:] = kn2_ref[0]
        vo_ref[0, keep:] = vn2_ref[0]
        for h in range(nh):
            sl, rs = head(h)
            l = l_sc[rs]
            o_ref[0, :, sl] = acc_sc[rs] / l
            lse_ref[0, :, sl] = jnp.broadcast_to(m_sc[rs] + jnp.log(l), (t_new, ATT_HEAD_DIM))


def _attn_sample(proj3, batch0, group, dilation, k_buf, v_buf):
    db, rows_total, _ = k_buf.shape
    nh = ATT_SLOTS
    length = rows_total // nh
    t_new = proj3.shape[1]
    ngroups = len(ATT_GROUPS)
    tl = _tile(length, 1024, SUBLANES)
    nlt = length // tl
    step = tl // t_new
    last = length // t_new - 1
    shift = t_new * nh

    def new_rows(col):
        c0 = col * ATT_OUT_WIDTH
        return proj3[batch0:, :, c0:c0 + ATT_OUT_WIDTH].reshape(db, shift, ATT_HEAD_DIM)

    def new_spec(col):
        return pl.BlockSpec((1, t_new, ATT_OUT_WIDTH), lambda b, l: (batch0 + b, 0, col))

    new2_spec = pl.BlockSpec((1, shift, ATT_HEAD_DIM), lambda b, l: (b, 0, 0))
    buf_spec = pl.BlockSpec((1, tl * nh, ATT_HEAD_DIM), lambda b, l: (b, l, 0))
    next_spec = pl.BlockSpec((1, shift, ATT_HEAD_DIM),
                             lambda b, l: (b, jnp.minimum((l + 1) * step, last), 0))
    small = pl.BlockSpec((1, t_new, ATT_OUT_WIDTH), lambda b, l: (b, 0, 0))
    small_shape = jax.ShapeDtypeStruct((db, t_new, ATT_OUT_WIDTH), F32)
    buf_shape = jax.ShapeDtypeStruct(k_buf.shape, F32)
    rows = nh * t_new
    return pl.pallas_call(
        functools.partial(_attn_sample_kernel, dilation=dilation, t_new=t_new),
        grid=(db, nlt),
        in_specs=[new_spec(group), new_spec(ngroups + group), new_spec(2 * ngroups + group),
                  new2_spec, new2_spec, buf_spec, buf_spec, next_spec, next_spec],
        out_specs=[small, small, buf_spec, buf_spec],
        out_shape=[small_shape, small_shape, buf_shape, buf_shape],
        scratch_shapes=[pltpu.VMEM((rows, 1), F32), pltpu.VMEM((rows, 1), F32),
                        pltpu.VMEM((rows, ATT_HEAD_DIM), F32)],
        compiler_params=_params("parallel", "arbitrary"),
        name=f"attn_sample_d{dilation}",
    )(proj3, proj3, proj3, new_rows(ngroups + group), new_rows(2 * ngroups + group),
      k_buf, v_buf, k_buf, v_buf)


def _combine_kernel(o0, o1, o2, l0, l1, l2, out_ref):
    a, b, c = l0[...], l1[...], l2[...]
    m = jnp.maximum(jnp.maximum(a, b), c)
    ea, eb, ec = jnp.exp(a - m), jnp.exp(b - m), jnp.exp(c - m)
    num = ea * o0[...] + eb * o1[...] + ec * o2[...]
    out_ref[...] = (num / (ea + eb + ec)).astype(out_ref.dtype)


def _combine(outs, lses):
    m, w = outs[0].shape
    tm = _tile(m, 512, SUBLANES)
    spec = pl.BlockSpec((tm, w), lambda i: (i, 0))
    return pl.pallas_call(
        _combine_kernel,
        grid=(m // tm,),
        in_specs=[spec] * 6,
        out_specs=spec,
        out_shape=jax.ShapeDtypeStruct((m, w), BF16),
        compiler_params=_params("parallel"),
        name="attn_combine",
    )(*outs, *lses)


def _cumsum_rows(x):
    c = x.shape[0]
    r = lax.broadcasted_iota(jnp.int32, (c, c), 0)
    s = lax.broadcasted_iota(jnp.int32, (c, c), 1)
    tri = jnp.where(r >= s, 1.0, 0.0).astype(BF16)
    hi = x.astype(BF16)
    rem = x - hi.astype(F32)
    mid = rem.astype(BF16)
    lo = (rem - mid.astype(F32)).astype(BF16)
    return _dot(tri, hi) + _dot(tri, mid) + _dot(tri, lo)


def _lower_bound(lb_ref):
    a = lb_ref[...]
    e = jnp.exp(a - jnp.max(a, axis=0, keepdims=True))
    return e[0:1] / jnp.sum(e, axis=0, keepdims=True)


def _hgrn_chunk(qg, fg, ig, gg, lb, gain, st):
    c = qg.shape[0]
    q = qg * _sigmoid(qg) * HG_SCALE
    forget = lb + (1.0 - lb) * _sigmoid(fg)
    k = 1.0 - forget
    v = ig
    b = _cumsum_rows(jnp.log(forget))
    v16 = v.astype(BF16)

    nsub = c // HG_SUB
    b3 = b.reshape(nsub, HG_SUB, LANES)
    q3 = q.reshape(nsub, HG_SUB, LANES)
    k3 = k.reshape(nsub, HG_SUB, LANES)
    pos = lax.broadcasted_iota(jnp.int32, (nsub, HG_SUB, LANES), 1)
    row = lax.broadcasted_iota(jnp.int32, (c, c), 0)
    col = lax.broadcasted_iota(jnp.int32, (c, c), 1)
    scores = jnp.zeros((c, c), F32)
    for s in range(HG_SUB):
        diff = jnp.where(pos >= s, b3 - b3[:, s:s + 1, :], NEG_BIG)
        w = jnp.sum(q3 * k3[:, s:s + 1, :] * jnp.exp(diff), axis=-1, keepdims=True)
        w = jnp.broadcast_to(w.reshape(c, 1), (c, c))
        scores = jnp.where(col == (row // HG_SUB) * HG_SUB + s, w, scores)

    width = HG_SUB
    while width < c:
        pair = 2 * width
        bm = jnp.concatenate(
            [jnp.broadcast_to(b[p * pair + width - 1:p * pair + width], (pair, LANES))
             for p in range(c // pair)], axis=0)
        e = jnp.exp(-jnp.abs(b - bm))
        right = (lax.broadcasted_iota(jnp.int32, (c, LANES), 0) // width) % 2 == 1
        ql = jnp.where(right, q * e, 0.0).astype(BF16)
        kl = jnp.where(right, 0.0, k * e).astype(BF16)
        a = _dot_nt(ql, kl)
        scores = jnp.where((row // pair == col // pair) & (row // width != col // width) & (row > col),
                           a, scores)
        width = pair

    o = _dot(scores.astype(BF16), v16) + _dot_nt((q * jnp.exp(b)).astype(BF16), st.astype(BF16))
    b_last = b[c - 1:c]
    st_new = st * jnp.exp(b_last) + _dot_tn(v16, (k * jnp.exp(b_last - b)).astype(BF16))
    out = _rms(o, gain) * (gg * _sigmoid(gg))
    return out, st_new


def _hgrn_prompt_kernel(q_ref, f_ref, i_ref, g_ref, lb_ref, gain_ref, o_ref, s_ref, st_sc):
    tb = pl.program_id(1)

    @pl.when(tb == 0)
    def _():
        st_sc[...] = jnp.zeros_like(st_sc)

    lb = _lower_bound(lb_ref)
    gain = gain_ref[...]
    st = st_sc[...]
    for ci in range(q_ref.shape[0] // HG_CHUNK):
        rs = slice(ci * HG_CHUNK, (ci + 1) * HG_CHUNK)
        out, st = _hgrn_chunk(q_ref[rs, :], f_ref[rs, :], i_ref[rs, :], g_ref[rs, :], lb, gain, st)
        o_ref[rs, :] = out.astype(o_ref.dtype)
    st_sc[...] = st

    @pl.when(tb == pl.num_programs(1) - 1)
    def _():
        s_ref[0] = st.T


def _hgrn_prompt(proj, seq, hg_lower_bound, hg_norm, hg_width):
    heads = hg_width // HG_EXPAND
    base = 3 * ATT_WIDTH // LANES
    per = hg_width // LANES
    tb = _tile(seq, 512, HG_CHUNK)

    def col(which):
        return pl.BlockSpec((tb, LANES), lambda h, t: (t, base + which * per + h))

    depth1 = hg_lower_bound.shape[0]
    return pl.pallas_call(
        _hgrn_prompt_kernel,
        grid=(heads, seq // tb),
        in_specs=[col(0), col(1), col(2), col(3),
                  pl.BlockSpec((depth1, LANES), lambda h, t: (0, h)),
                  pl.BlockSpec((1, LANES), lambda h, t: (0, 0))],
        out_specs=[pl.BlockSpec((tb, LANES), lambda h, t: (t, h)),
                   pl.BlockSpec((1, HG_EXPAND, HG_HEAD_V), lambda h, t: (h, 0, 0))],
        out_shape=[jax.ShapeDtypeStruct((seq, hg_width), BF16),
                   jax.ShapeDtypeStruct((heads, HG_EXPAND, HG_HEAD_V), F32)],
        scratch_shapes=[pltpu.VMEM((HG_HEAD_V, HG_EXPAND), F32)],
        compiler_params=_params("parallel", "arbitrary"),
        name="hgrn_prompt",
    )(proj, proj, proj, proj, hg_lower_bound, hg_norm.reshape(1, LANES))


def _hgrn_sample_kernel(q_ref, f_ref, i_ref, g_ref, lb_ref, gain_ref, s_ref, o_ref, so_ref):
    gain = gain_ref[...]
    for h in range(s_ref.shape[1]):
        sl = slice(h * LANES, (h + 1) * LANES)
        lb = _lower_bound(lb_ref.at[:, sl])
        out, st = _hgrn_chunk(q_ref[0, :, sl], f_ref[0, :, sl], i_ref[0, :, sl], g_ref[0, :, sl],
                              lb, gain, s_ref[0, h].T)
        o_ref[0, :, sl] = out.astype(o_ref.dtype)
        so_ref[0, h] = st.T


def _hgrn_sample(proj3, batch0, state, hg_lower_bound, hg_norm):
    db, heads = state.shape[:2]
    t_new = proj3.shape[1]
    hg_width = heads * HG_EXPAND
    cw = _tile(hg_width, ATT_OUT_WIDTH, LANES)
    assert 3 * ATT_WIDTH % cw == 0
    base = 3 * ATT_WIDTH // cw
    per = hg_width // cw
    hb = cw // HG_EXPAND

    def col(which):
        return pl.BlockSpec((1, t_new, cw), lambda b, c: (batch0 + b, 0, base + which * per + c))

    depth1 = hg_lower_bound.shape[0]
    st_spec = pl.BlockSpec((1, hb, HG_EXPAND, HG_HEAD_V), lambda b, c: (b, c, 0, 0))
    return pl.pallas_call(
        _hgrn_sample_kernel,
        grid=(db, per),
        in_specs=[col(0), col(1), col(2), col(3),
                  pl.BlockSpec((depth1, cw), lambda b, c: (0, c)),
                  pl.BlockSpec((1, LANES), lambda b, c: (0, 0)),
                  st_spec],
        out_specs=[pl.BlockSpec((1, t_new, cw), lambda b, c: (b, 0, c)), st_spec],
        out_shape=[jax.ShapeDtypeStruct((db, t_new, hg_width), BF16),
                   jax.ShapeDtypeStruct(state.shape, F32)],
        compiler_params=_params("parallel", "parallel"),
        name="hgrn_sample",
    )(proj3, proj3, proj3, proj3, hg_lower_bound, hg_norm.reshape(1, LANES), state)


def kernel(x_prompt, x_sample, cache_k_w128, cache_v_w128, cache_k_w512, cache_v_w512,
           cache_k_w2048, cache_v_w2048, state_hgrn, hg_lower_bound, w_in, w_gate, b_gate,
           w_proj_att, w_proj_hg, w_out, hg_norm, norm_mix_pre, norm_mix_post,
           norm_ffn_pre, norm_ffn_post, w_up, w_down):
    assert w_in.shape[0] == 1, "single-layer trunk"
    batch, seq, d_model = x_prompt.shape
    db, t_new, _ = x_sample.shape
    assert batch == 1
    hg_width = w_proj_hg.shape[1]
    n_prompt = batch * seq
    n_sample = db * t_new
    caches = (cache_k_w128, cache_v_w128, cache_k_w512, cache_v_w512, cache_k_w2048, cache_v_w2048)

    xp = x_prompt.reshape(n_prompt, d_model)
    xs = x_sample.reshape(n_sample, d_model)
    h = _norm_cast(xp, xs, norm_mix_pre[0])
    proj = _matmul(h, w_in[0].astype(BF16), name="in_proj")
    gates = _matmul(h, w_gate[0].astype(BF16), bias=b_gate[0], name="gate_proj")
    in_width = proj.shape[1]
    proj3 = proj.reshape((n_prompt + n_sample) // t_new, t_new, in_width)
    batch0 = n_prompt // t_new

    outs_p, lses_p, outs_s, lses_s, new_kv_p, new_kv_s = [], [], [], [], [], []
    for g, (window, dilation) in enumerate(ATT_GROUPS):
        o, lse = _attn_prompt(proj, seq, g, dilation)
        outs_p.append(o)
        lses_p.append(lse)
        keep = min(window, seq)
        for part in (1, 2):
            c0 = part * ATT_WIDTH + g * ATT_OUT_WIDTH
            rows = proj[n_prompt - keep:n_prompt, c0:c0 + ATT_OUT_WIDTH]
            new_kv_p.append(rows.reshape(1, batch, keep, ATT_SLOTS, ATT_HEAD_DIM))
        k_buf, v_buf = caches[2 * g][0], caches[2 * g + 1][0]
        length = k_buf.shape[1]
        assert length == window and length == dilation * ATT_BLOCK
        o, lse, k_new, v_new = _attn_sample(proj3, batch0, g, dilation,
                                            k_buf.reshape(db, length * ATT_SLOTS, ATT_HEAD_DIM),
                                            v_buf.reshape(db, length * ATT_SLOTS, ATT_HEAD_DIM))
        outs_s.append(o.reshape(n_sample, ATT_OUT_WIDTH))
        lses_s.append(lse.reshape(n_sample, ATT_OUT_WIDTH))
        new_kv_s += [k_new.reshape(1, db, length, ATT_SLOTS, ATT_HEAD_DIM),
                     v_new.reshape(1, db, length, ATT_SLOTS, ATT_HEAD_DIM)]
    att = jnp.concatenate([_combine(outs_p, lses_p), _combine(outs_s, lses_s)], axis=0)

    hg_p, state_p = _hgrn_prompt(proj, seq, hg_lower_bound, hg_norm[0], hg_width)
    hg_s, state_s = _hgrn_sample(proj3, batch0, state_hgrn[0], hg_lower_bound, hg_norm[0])
    hg = jnp.concatenate([hg_p, hg_s.reshape(n_sample, hg_width)], axis=0)

    merged = _merge_proj(att, hg, w_proj_att[0].astype(BF16), w_proj_hg[0].astype(BF16), gates)
    mixed = _matmul(merged, w_out[0].astype(BF16), name="out_proj")
    x1, h2 = _resid_norm(xp, xs, mixed, norm_mix_post[0], norm_ffn_pre[0])

    u = _matmul(h2, w_up[0].astype(BF16), out_dtype=BF16, act="relu2", name="ffn_up")
    z = _matmul_ksplit(u, w_down[0].astype(BF16), tk=4096, name="ffn_down")
    y_prompt, y_sample = _resid_final(x1, z, norm_ffn_post[0], n_prompt)
    y_prompt = y_prompt.reshape(batch, seq, d_model)
    y_sample = y_sample.reshape(db, t_new, d_model)
    heads = hg_width // HG_EXPAND
    return (y_prompt, y_sample, *new_kv_p,
            state_p.reshape(1, batch, heads, HG_EXPAND, HG_HEAD_V),
            *new_kv_s, state_s[None])
```

```python
import functools
import math

import jax
import jax.numpy as jnp
from jax import lax
from jax.experimental import pallas as pl
from jax.experimental.pallas import tpu as pltpu

ATT_HEAD_DIM = 128
ATT_SLOTS = 8
ATT_GROUPS = ((128, 1), (512, 4), (2048, 16))
ATT_BLOCK = 128
ATT_OUT_WIDTH = ATT_SLOTS * ATT_HEAD_DIM
ATT_WIDTH = len(ATT_GROUPS) * ATT_OUT_WIDTH
ATT_SCALE = ATT_HEAD_DIM ** -0.5
HG_EXPAND = 128
HG_HEAD_V = 128
HG_SCALE = HG_EXPAND ** -0.5
HG_CHUNK = 64
HG_SUB = 8
RMS_EPS = 1e-6
NEG_BIG = -1e30

V7X_VMEM_BYTES = 64 * 1024 * 1024
VMEM_LIMIT = V7X_VMEM_BYTES - 8 * 1024 * 1024
LANES = 128
SUBLANES = 8

BF16 = jnp.bfloat16
F32 = jnp.float32


def _params(*sem):
    return pltpu.CompilerParams(dimension_semantics=sem, vmem_limit_bytes=VMEM_LIMIT)


def _tile(n, target, mult):
    if n <= target:
        return n
    t = (target // mult) * mult
    while t >= mult:
        if n % t == 0:
            return t
        t -= mult
    raise ValueError(f"no tile for {n}")


def _sigmoid(x):
    return 1.0 / (1.0 + jnp.exp(-x))


def _dot(a, b):
    return jnp.dot(a, b, preferred_element_type=F32)


def _dot_nt(a, b):
    return lax.dot_general(a, b, (((1,), (1,)), ((), ())), preferred_element_type=F32)


def _dot_tn(a, b):
    return lax.dot_general(a, b, (((0,), (0,)), ((), ())), preferred_element_type=F32)


def _rms(x, gain):
    return x * lax.rsqrt(jnp.mean(x * x, axis=-1, keepdims=True) + RMS_EPS) * gain


def _split_rows(n_prompt, n_sample, d, target):
    tm = _tile(math.gcd(n_prompt, n_sample), target, SUBLANES)
    np_tiles = n_prompt // tm
    prompt = pl.BlockSpec((tm, d), lambda i: (jnp.minimum(i, np_tiles - 1), 0))
    sample = pl.BlockSpec((tm, d), lambda i: (jnp.maximum(i - np_tiles, 0), 0))
    stacked = pl.BlockSpec((tm, d), lambda i: (i, 0))
    vec = pl.BlockSpec((1, d), lambda i: (0, 0))
    return tm, np_tiles, prompt, sample, stacked, vec


def _norm_cast_kernel(xp_ref, xs_ref, g_ref, o_ref, *, np_tiles):
    i = pl.program_id(0)

    @pl.when(i < np_tiles)
    def _():
        o_ref[...] = _rms(xp_ref[...], g_ref[...]).astype(o_ref.dtype)

    @pl.when(i >= np_tiles)
    def _():
        o_ref[...] = _rms(xs_ref[...], g_ref[...]).astype(o_ref.dtype)


def _norm_cast(xp, xs, gain):
    d = xp.shape[1]
    m = xp.shape[0] + xs.shape[0]
    tm, np_tiles, prompt, sample, stacked, vec = _split_rows(xp.shape[0], xs.shape[0], d, 256)
    return pl.pallas_call(
        functools.partial(_norm_cast_kernel, np_tiles=np_tiles),
        grid=(m // tm,),
        in_specs=[prompt, sample, vec],
        out_specs=stacked,
        out_shape=jax.ShapeDtypeStruct((m, d), BF16),
        compiler_params=_params("arbitrary"),
        name="norm_cast",
    )(xp, xs, gain.reshape(1, d))


def _resid_norm_kernel(xp_ref, xs_ref, y_ref, gpost_ref, gpre_ref, x1_ref, h_ref, *, np_tiles):
    i = pl.program_id(0)

    def body(x_ref):
        x1 = x_ref[...] + _rms(y_ref[...], gpost_ref[...])
        x1_ref[...] = x1
        h_ref[...] = _rms(x1, gpre_ref[...]).astype(h_ref.dtype)

    pl.when(i < np_tiles)(lambda: body(xp_ref))
    pl.when(i >= np_tiles)(lambda: body(xs_ref))


def _resid_norm(xp, xs, y, gain_post, gain_pre):
    m, d = y.shape
    tm, np_tiles, prompt, sample, stacked, vec = _split_rows(xp.shape[0], xs.shape[0], d, 256)
    return pl.pallas_call(
        functools.partial(_resid_norm_kernel, np_tiles=np_tiles),
        grid=(m // tm,),
        in_specs=[prompt, sample, stacked, vec, vec],
        out_specs=[stacked, stacked],
        out_shape=[jax.ShapeDtypeStruct((m, d), F32), jax.ShapeDtypeStruct((m, d), BF16)],
        compiler_params=_params("arbitrary"),
        name="resid_norm",
    )(xp, xs, y, gain_post.reshape(1, d), gain_pre.reshape(1, d))


def _resid_final_kernel(x_ref, y_ref, g_ref, op_ref, os_ref, *, np_tiles):
    i = pl.program_id(0)

    @pl.when(i < np_tiles)
    def _():
        op_ref[...] = x_ref[...] + _rms(y_ref[...], g_ref[...])

    @pl.when(i >= np_tiles)
    def _():
        os_ref[...] = x_ref[...] + _rms(y_ref[...], g_ref[...])


def _resid_final(x, y, gain, n_prompt):
    m, d = x.shape
    tm, np_tiles, prompt, sample, stacked, vec = _split_rows(n_prompt, m - n_prompt, d, 256)
    return pl.pallas_call(
        functools.partial(_resid_final_kernel, np_tiles=np_tiles),
        grid=(m // tm,),
        in_specs=[stacked, stacked, vec],
        out_specs=[prompt, sample],
        out_shape=[jax.ShapeDtypeStruct((n_prompt, d), F32),
                   jax.ShapeDtypeStruct((m - n_prompt, d), F32)],
        compiler_params=_params("arbitrary"),
        name="resid_final",
    )(x, y, gain.reshape(1, d))


def _mm_kernel(a_ref, w_ref, o_ref, *, act):
    acc = _dot(a_ref[...], w_ref[...])
    if act == "relu2":
        acc = jnp.maximum(acc, 0.0)
        acc = acc * acc
    o_ref[...] = acc.astype(o_ref.dtype)


def _mm_bias_sigmoid_kernel(a_ref, w_ref, b_ref, o_ref):
    o_ref[...] = _sigmoid(_dot(a_ref[...], w_ref[...]) + b_ref[...]).astype(o_ref.dtype)


def _matmul(a, w, *, out_dtype=F32, act=None, bias=None, name="matmul"):
    m, k = a.shape
    n = w.shape[1]
    tm = _tile(m, 1024, SUBLANES)
    tn = _tile(n, 1024, LANES)
    a_spec = pl.BlockSpec((tm, k), lambda i, j: (i, 0))
    w_spec = pl.BlockSpec((k, tn), lambda i, j: (0, j))
    o_spec = pl.BlockSpec((tm, tn), lambda i, j: (i, j))
    if bias is None:
        body, ins, specs = functools.partial(_mm_kernel, act=act), (a, w), [a_spec, w_spec]
    else:
        body, ins = _mm_bias_sigmoid_kernel, (a, w, bias.reshape(1, n))
        specs = [a_spec, w_spec, pl.BlockSpec((1, tn), lambda i, j: (0, j))]
    return pl.pallas_call(
        body,
        grid=(m // tm, n // tn),
        in_specs=specs,
        out_specs=o_spec,
        out_shape=jax.ShapeDtypeStruct((m, n), out_dtype),
        compiler_params=_params("parallel", "arbitrary"),
        name=name,
    )(*ins)


def _mm_acc_kernel(a_ref, w_ref, o_ref):
    @pl.when(pl.program_id(2) == 0)
    def _():
        o_ref[...] = jnp.zeros_like(o_ref)

    o_ref[...] += _dot(a_ref[...], w_ref[...])


def _matmul_ksplit(a, w, *, tk, name):
    m, k = a.shape
    n = w.shape[1]
    tm = _tile(m, 1024, SUBLANES)
    tn = _tile(n, 1024, LANES)
    tk = _tile(k, tk, LANES)
    return pl.pallas_call(
        _mm_acc_kernel,
        grid=(m // tm, n // tn, k // tk),
        in_specs=[pl.BlockSpec((tm, tk), lambda i, j, l: (i, l)),
                  pl.BlockSpec((tk, tn), lambda i, j, l: (l, j))],
        out_specs=pl.BlockSpec((tm, tn), lambda i, j, l: (i, j)),
        out_shape=jax.ShapeDtypeStruct((m, n), F32),
        compiler_params=_params("parallel", "arbitrary", "arbitrary"),
        name=name,
    )(a, w)


def _merge_kernel(att_ref, hg_ref, wa_ref, wh_ref, ga_ref, gh_ref, o_ref):
    pa = _dot(att_ref[...], wa_ref[...])
    ph = _dot(hg_ref[...], wh_ref[...])
    o_ref[...] = (ga_ref[...] * pa + gh_ref[...] * ph).astype(o_ref.dtype)


def _merge_proj(att, hg, w_att, w_hg, gates):
    m, ka = att.shape
    kh = hg.shape[1]
    d = w_att.shape[1]
    tm = _tile(m, 1024, SUBLANES)
    tn = _tile(d, 1024, LANES)
    nj = d // tn
    return pl.pallas_call(
        _merge_kernel,
        grid=(m // tm, nj),
        in_specs=[pl.BlockSpec((tm, ka), lambda i, j: (i, 0)),
                  pl.BlockSpec((tm, kh), lambda i, j: (i, 0)),
                  pl.BlockSpec((ka, tn), lambda i, j: (0, j)),
                  pl.BlockSpec((kh, tn), lambda i, j: (0, j)),
                  pl.BlockSpec((tm, tn), lambda i, j: (i, j)),
                  pl.BlockSpec((tm, tn), lambda i, j: (i, j + nj))],
        out_specs=pl.BlockSpec((tm, tn), lambda i, j: (i, j)),
        out_shape=jax.ShapeDtypeStruct((m, d), BF16),
        compiler_params=_params("parallel", "arbitrary"),
        name="merge_proj",
    )(att, hg, w_att, w_hg, gates, gates)


def _attn_prompt_kernel(q_ref, kp_ref, kc_ref, vp_ref, vc_ref, o_ref, lse_ref, *, dilation):
    jb = pl.program_id(1)
    blk = ATT_BLOCK
    r = lax.broadcasted_iota(jnp.int32, (blk, 2 * blk), 0)
    c = lax.broadcasted_iota(jnp.int32, (blk, 2 * blk), 1)
    dist = blk + r - c
    mask = (dist >= 0) & (dist <= blk) & ((c >= blk) | (jb > 0))
    for res in range(dilation):
        rows = pl.ds(res, blk, stride=dilation) if dilation > 1 else slice(None)
        for h in range(q_ref.shape[1] // ATT_HEAD_DIM):
            sl = slice(h * ATT_HEAD_DIM, (h + 1) * ATT_HEAD_DIM)
            q = q_ref[rows, sl].astype(BF16)
            k = jnp.concatenate([kp_ref[rows, sl], kc_ref[rows, sl]], axis=0).astype(BF16)
            v = jnp.concatenate([vp_ref[rows, sl], vc_ref[rows, sl]], axis=0).astype(BF16)
            s = jnp.where(mask, _dot_nt(q, k) * ATT_SCALE, NEG_BIG)
            m = jnp.max(s, axis=-1, keepdims=True)
            p = jnp.exp(s - m)
            l = jnp.sum(p, axis=-1, keepdims=True)
            o_ref[rows, sl] = _dot(p.astype(BF16), v) / l
            lse_ref[rows, sl] = jnp.broadcast_to(m + jnp.log(l), (blk, ATT_HEAD_DIM))


def _attn_prompt(proj, seq, group, dilation):
    chunk = ATT_BLOCK * dilation
    nch = seq // chunk
    lw = ATT_OUT_WIDTH if dilation == 1 else ATT_HEAD_DIM
    nhb = ATT_OUT_WIDTH // lw
    ngroups = len(ATT_GROUPS)

    def spec(col, prev):
        c0 = col * nhb
        if prev:
            return pl.BlockSpec((chunk, lw), lambda hb, j: (jnp.maximum(j - 1, 0), c0 + hb))
        return pl.BlockSpec((chunk, lw), lambda hb, j: (j, c0 + hb))

    out_spec = pl.BlockSpec((chunk, lw), lambda hb, j: (j, hb))
    shape = jax.ShapeDtypeStruct((seq, ATT_OUT_WIDTH), F32)
    return pl.pallas_call(
        functools.partial(_attn_prompt_kernel, dilation=dilation),
        grid=(nhb, nch),
        in_specs=[spec(group, False),
                  spec(ngroups + group, True), spec(ngroups + group, False),
                  spec(2 * ngroups + group, True), spec(2 * ngroups + group, False)],
        out_specs=[out_spec, out_spec],
        out_shape=[shape, shape],
        compiler_params=_params("parallel", "arbitrary"),
        name=f"attn_prompt_d{dilation}",
    )(proj, proj, proj, proj, proj)


def _attn_sample_kernel(q_ref, kn_ref, vn_ref, kn2_ref, vn2_ref, kb_ref, vb_ref, kx_ref, vx_ref,
                        o_ref, lse_ref, ko_ref, vo_ref, m_sc, l_sc, acc_sc, *, dilation, t_new):
    lt = pl.program_id(1)
    nlt = pl.num_programs(1)
    nh = ATT_SLOTS
    tl = kb_ref.shape[1] // nh
    shift = t_new * nh
    dmask = dilation - 1

    def head(h):
        return slice(h * ATT_HEAD_DIM, (h + 1) * ATT_HEAD_DIM), slice(h * t_new, (h + 1) * t_new)

    @pl.when(lt == 0)
    def _():
        t = lax.broadcasted_iota(jnp.int32, (t_new, t_new), 0)
        i = lax.broadcasted_iota(jnp.int32, (t_new, t_new), 1)
        mask = (i <= t) & (((t - i) & dmask) == 0)
        for h in range(nh):
            sl, rs = head(h)
            s = _dot_nt(q_ref[0, :, sl].astype(BF16), kn_ref[0, :, sl].astype(BF16)) * ATT_SCALE
            s = jnp.where(mask, s, NEG_BIG)
            m = jnp.max(s, axis=-1, keepdims=True)
            p = jnp.exp(s - m)
            m_sc[rs] = m
            l_sc[rs] = jnp.sum(p, axis=-1, keepdims=True)
            acc_sc[rs] = _dot(p.astype(BF16), vn_ref[0, :, sl].astype(BF16))

    t = lax.broadcasted_iota(jnp.int32, (t_new, tl), 0)
    j = lax.broadcasted_iota(jnp.int32, (t_new, tl), 1) + lt * tl
    mask = (j >= t) & (((j - t) & dmask) == 0)
    for h in range(nh):
        sl, rs = head(h)
        kh = kb_ref[0, pl.ds(h, tl, stride=nh), :].astype(BF16)
        vh = vb_ref[0, pl.ds(h, tl, stride=nh), :].astype(BF16)
        s = jnp.where(mask, _dot_nt(q_ref[0, :, sl].astype(BF16), kh) * ATT_SCALE, NEG_BIG)
        m_old = m_sc[rs]
        m_new = jnp.maximum(m_old, jnp.max(s, axis=-1, keepdims=True))
        alpha = jnp.exp(m_old - m_new)
        p = jnp.exp(s - m_new)
        l_sc[rs] = alpha * l_sc[rs] + jnp.sum(p, axis=-1, keepdims=True)
        acc_sc[rs] = alpha * acc_sc[rs] + _dot(p.astype(BF16), vh)
        m_sc[rs] = m_new

    keep = tl * nh - shift
    ko_ref[0, :keep] = kb_ref[0, shift:]
    vo_ref[0, :keep] = vb_ref[0, shift:]

    @pl.when(lt < nlt - 1)
    def _():
        ko_ref[0, keep:] = kx_ref[0]
        vo_ref[0, keep:] = vx_ref[0]

    @pl.when(lt == nlt - 1)
    def _():
        ko_ref[0, keep:] = kn2_ref[0]
        vo_ref[0, keep:] = vn2_ref[0]
        for h in range(nh):
            sl, rs = head(h)
            l = l_sc[rs]
            o_ref[0, :, sl] = acc_sc[rs] / l
            lse_ref[0, :, sl] = jnp.broadcast_to(m_sc[rs] + jnp.log(l), (t_new, ATT_HEAD_DIM))


def _attn_sample(proj3, batch0, group, dilation, k_buf, v_buf):
    db, rows_total, _ = k_buf.shape
    nh = ATT_SLOTS
    length = rows_total // nh
    t_new = proj3.shape[1]
    ngroups = len(ATT_GROUPS)
    tl = _tile(length, 1024, SUBLANES)
    nlt = length // tl
    step = tl // t_new
    last = length // t_new - 1
    shift = t_new * nh

    def new_rows(col):
        c0 = col * ATT_OUT_WIDTH
        return proj3[batch0:, :, c0:c0 + ATT_OUT_WIDTH].reshape(db, shift, ATT_HEAD_DIM)

    def new_spec(col):
        return pl.BlockSpec((1, t_new, ATT_OUT_WIDTH), lambda b, l: (batch0 + b, 0, col))

    new2_spec = pl.BlockSpec((1, shift, ATT_HEAD_DIM), lambda b, l: (b, 0, 0))
    buf_spec = pl.BlockSpec((1, tl * nh, ATT_HEAD_DIM), lambda b, l: (b, l, 0))
    next_spec = pl.BlockSpec((1, shift, ATT_HEAD_DIM),
                             lambda b, l: (b, jnp.minimum((l + 1) * step, last), 0))
    small = pl.BlockSpec((1, t_new, ATT_OUT_WIDTH), lambda b, l: (b, 0, 0))
    small_shape = jax.ShapeDtypeStruct((db, t_new, ATT_OUT_WIDTH), F32)
    buf_shape = jax.ShapeDtypeStruct(k_buf.shape, F32)
    rows = nh * t_new
    return pl.pallas_call(
        functools.partial(_attn_sample_kernel, dilation=dilation, t_new=t_new),
        grid=(db, nlt),
        in_specs=[new_spec(group), new_spec(ngroups + group), new_spec(2 * ngroups + group),
                  new2_spec, new2_spec, buf_spec, buf_spec, next_spec, next_spec],
        out_specs=[small, small, buf_spec, buf_spec],
        out_shape=[small_shape, small_shape, buf_shape, buf_shape],
        scratch_shapes=[pltpu.VMEM((rows, 1), F32), pltpu.VMEM((rows, 1), F32),
                        pltpu.VMEM((rows, ATT_HEAD_DIM), F32)],
        compiler_params=_params("parallel", "arbitrary"),
        name=f"attn_sample_d{dilation}",
    )(proj3, proj3, proj3, new_rows(ngroups + group), new_rows(2 * ngroups + group),
      k_buf, v_buf, k_buf, v_buf)


def _combine_kernel(o0, o1, o2, l0, l1, l2, out_ref):
    a, b, c = l0[...], l1[...], l2[...]
    m = jnp.maximum(jnp.maximum(a, b), c)
    ea, eb, ec = jnp.exp(a - m), jnp.exp(b - m), jnp.exp(c - m)
    num = ea * o0[...] + eb * o1[...] + ec * o2[...]
    out_ref[...] = (num / (ea + eb + ec)).astype(out_ref.dtype)


def _combine(outs, lses):
    m, w = outs[0].shape
    tm = _tile(m, 512, SUBLANES)
    spec = pl.BlockSpec((tm, w), lambda i: (i, 0))
    return pl.pallas_call(
        _combine_kernel,
        grid=(m // tm,),
        in_specs=[spec] * 6,
        out_specs=spec,
        out_shape=jax.ShapeDtypeStruct((m, w), BF16),
        compiler_params=_params("parallel"),
        name="attn_combine",
    )(*outs, *lses)


def _cumsum_rows(x):
    c = x.shape[0]
    r = lax.broadcasted_iota(jnp.int32, (c, c), 0)
    s = lax.broadcasted_iota(jnp.int32, (c, c), 1)
    tri = jnp.where(r >= s, 1.0, 0.0).astype(BF16)
    hi = x.astype(BF16)
    rem = x - hi.astype(F32)
    mid = rem.astype(BF16)
    lo = (rem - mid.astype(F32)).astype(BF16)
    return _dot(tri, hi) + _dot(tri, mid) + _dot(tri, lo)


def _lower_bound(lb_ref):
    a = lb_ref[...]
    e = jnp.exp(a - jnp.max(a, axis=0, keepdims=True))
    return e[0:1] / jnp.sum(e, axis=0, keepdims=True)


def _hgrn_chunk(qg, fg, ig, gg, lb, gain, st):
    c = qg.shape[0]
    q = qg * _sigmoid(qg) * HG_SCALE
    forget = lb + (1.0 - lb) * _sigmoid(fg)
    k = 1.0 - forget
    v = ig
    b = _cumsum_rows(jnp.log(forget))
    v16 = v.astype(BF16)

    nsub = c // HG_SUB
    b3 = b.reshape(nsub, HG_SUB, LANES)
    q3 = q.reshape(nsub, HG_SUB, LANES)
    k3 = k.reshape(nsub, HG_SUB, LANES)
    pos = lax.broadcasted_iota(jnp.int32, (nsub, HG_SUB, LANES), 1)
    row = lax.broadcasted_iota(jnp.int32, (c, c), 0)
    col = lax.broadcasted_iota(jnp.int32, (c, c), 1)
    scores = jnp.zeros((c, c), F32)
    for s in range(HG_SUB):
        diff = jnp.where(pos >= s, b3 - b3[:, s:s + 1, :], NEG_BIG)
        w = jnp.sum(q3 * k3[:, s:s + 1, :] * jnp.exp(diff), axis=-1, keepdims=True)
        w = jnp.broadcast_to(w.reshape(c, 1), (c, c))
        scores = jnp.where(col == (row // HG_SUB) * HG_SUB + s, w, scores)

    width = HG_SUB
    while width < c:
        pair = 2 * width
        bm = jnp.concatenate(
            [jnp.broadcast_to(b[p * pair + width - 1:p * pair + width], (pair, LANES))
             for p in range(c // pair)], axis=0)
        e = jnp.exp(-jnp.abs(b - bm))
        right = (lax.broadcasted_iota(jnp.int32, (c, LANES), 0) // width) % 2 == 1
        ql = jnp.where(right, q * e, 0.0).astype(BF16)
        kl = jnp.where(right, 0.0, k * e).astype(BF16)
        a = _dot_nt(ql, kl)
        scores = jnp.where((row // pair == col // pair) & (row // width != col // width) & (row > col),
                           a, scores)
        width = pair

    o = _dot(scores.astype(BF16), v16) + _dot_nt((q * jnp.exp(b)).astype(BF16), st.astype(BF16))
    b_last = b[c - 1:c]
    st_new = st * jnp.exp(b_last) + _dot_tn(v16, (k * jnp.exp(b_last - b)).astype(BF16))
    out = _rms(o, gain) * (gg * _sigmoid(gg))
    return out, st_new


def _hgrn_prompt_kernel(q_ref, f_ref, i_ref, g_ref, lb_ref, gain_ref, o_ref, s_ref, st_sc):
    tb = pl.program_id(1)

    @pl.when(tb == 0)
    def _():
        st_sc[...] = jnp.zeros_like(st_sc)

    lb = _lower_bound(lb_ref)
    gain = gain_ref[...]
    st = st_sc[...]
    for ci in range(q_ref.shape[0] // HG_CHUNK):
        rs = slice(ci * HG_CHUNK, (ci + 1) * HG_CHUNK)
        out, st = _hgrn_chunk(q_ref[rs, :], f_ref[rs, :], i_ref[rs, :], g_ref[rs, :], lb, gain, st)
        o_ref[rs, :] = out.astype(o_ref.dtype)
    st_sc[...] = st

    @pl.when(tb == pl.num_programs(1) - 1)
    def _():
        s_ref[0] = st.T


def _hgrn_prompt(proj, seq, hg_lower_bound, hg_norm, hg_width):
    heads = hg_width // HG_EXPAND
    base = 3 * ATT_WIDTH // LANES
    per = hg_width // LANES
    tb = _tile(seq, 512, HG_CHUNK)

    def col(which):
        return pl.BlockSpec((tb, LANES), lambda h, t: (t, base + which * per + h))

    depth1 = hg_lower_bound.shape[0]
    return pl.pallas_call(
        _hgrn_prompt_kernel,
        grid=(heads, seq // tb),
        in_specs=[col(0), col(1), col(2), col(3),
                  pl.BlockSpec((depth1, LANES), lambda h, t: (0, h)),
                  pl.BlockSpec((1, LANES), lambda h, t: (0, 0))],
        out_specs=[pl.BlockSpec((tb, LANES), lambda h, t: (t, h)),
                   pl.BlockSpec((1, HG_EXPAND, HG_HEAD_V), lambda h, t: (h, 0, 0))],
        out_shape=[jax.ShapeDtypeStruct((seq, hg_width), BF16),
                   jax.ShapeDtypeStruct((heads, HG_EXPAND, HG_HEAD_V), F32)],
        scratch_shapes=[pltpu.VMEM((HG_HEAD_V, HG_EXPAND), F32)],
        compiler_params=_params("parallel", "arbitrary"),
        name="hgrn_prompt",
    )(proj, proj, proj, proj, hg_lower_bound, hg_norm.reshape(1, LANES))


def _hgrn_sample_kernel(q_ref, f_ref, i_ref, g_ref, lb_ref, gain_ref, s_ref, o_ref, so_ref):
    gain = gain_ref[...]
    for h in range(s_ref.shape[1]):
        sl = slice(h * LANES, (h + 1) * LANES)
        lb = _lower_bound(lb_ref.at[:, sl])
        out, st = _hgrn_chunk(q_ref[0, :, sl], f_ref[0, :, sl], i_ref[0, :, sl], g_ref[0, :, sl],
                              lb, gain, s_ref[0, h].T)
        o_ref[0, :, sl] = out.astype(o_ref.dtype)
        so_ref[0, h] = st.T


def _hgrn_sample(proj3, batch0, state, hg_lower_bound, hg_norm):
    db, heads = state.shape[:2]
    t_new = proj3.shape[1]
    hg_width = heads * HG_EXPAND
    cw = _tile(hg_width, ATT_OUT_WIDTH, LANES)
    assert 3 * ATT_WIDTH % cw == 0
    base = 3 * ATT_WIDTH // cw
    per = hg_width // cw
    hb = cw // HG_EXPAND

    def col(which):
        return pl.BlockSpec((1, t_new, cw), lambda b, c: (batch0 + b, 0, base + which * per + c))

    depth1 = hg_lower_bound.shape[0]
    st_spec = pl.BlockSpec((1, hb, HG_EXPAND, HG_HEAD_V), lambda b, c: (b, c, 0, 0))
    return pl.pallas_call(
        _hgrn_sample_kernel,
        grid=(db, per),
        in_specs=[col(0), col(1), col(2), col(3),
                  pl.BlockSpec((depth1, cw), lambda b, c: (0, c)),
                  pl.BlockSpec((1, LANES), lambda b, c: (0, 0)),
                  st_spec],
        out_specs=[pl.BlockSpec((1, t_new, cw), lambda b, c: (b, 0, c)), st_spec],
        out_shape=[jax.ShapeDtypeStruct((db, t_new, hg_width), BF16),
                   jax.ShapeDtypeStruct(state.shape, F32)],
        compiler_params=_params("parallel", "parallel"),
        name="hgrn_sample",
    )(proj3, proj3, proj3, proj3, hg_lower_bound, hg_norm.reshape(1, LANES), state)


def kernel(x_prompt, x_sample, cache_k_w128, cache_v_w128, cache_k_w512, cache_v_w512,
           cache_k_w2048, cache_v_w2048, state_hgrn, hg_lower_bound, w_in, w_gate, b_gate,
           w_proj_att, w_proj_hg, w_out, hg_norm, norm_mix_pre, norm_mix_post,
           norm_ffn_pre, norm_ffn_post, w_up, w_down):
    assert w_in.shape[0] == 1, "single-layer trunk"
    batch, seq, d_model = x_prompt.shape
    db, t_new, _ = x_sample.shape
    assert batch == 1
    hg_width = w_proj_hg.shape[1]
    n_prompt = batch * seq
    n_sample = db * t_new
    caches = (cache_k_w128, cache_v_w128, cache_k_w512, cache_v_w512, cache_k_w2048, cache_v_w2048)

    xp = x_prompt.reshape(n_prompt, d_model)
    xs = x_sample.reshape(n_sample, d_model)
    h = _norm_cast(xp, xs, norm_mix_pre[0])
    proj = _matmul(h, w_in[0].astype(BF16), name="in_proj")
    gates = _matmul(h, w_gate[0].astype(BF16), bias=b_gate[0], name="gate_proj")
    in_width = proj.shape[1]
    proj3 = proj.reshape((n_prompt + n_sample) // t_new, t_new, in_width)
    batch0 = n_prompt // t_new

    outs_p, lses_p, outs_s, lses_s, new_kv_p, new_kv_s = [], [], [], [], [], []
    for g, (window, dilation) in enumerate(ATT_GROUPS):
        o, lse = _attn_prompt(proj, seq, g, dilation)
        outs_p.append(o)
        lses_p.append(lse)
        keep = min(window, seq)
        for part in (1, 2):
            c0 = part * ATT_WIDTH + g * ATT_OUT_WIDTH
            rows = proj[n_prompt - keep:n_prompt, c0:c0 + ATT_OUT_WIDTH]
            new_kv_p.append(rows.reshape(1, batch, keep, ATT_SLOTS, ATT_HEAD_DIM))
        k_buf, v_buf = caches[2 * g][0], caches[2 * g + 1][0]
        length = k_buf.shape[1]
        assert length == window and length == dilation * ATT_BLOCK
        o, lse, k_new, v_new = _attn_sample(proj3, batch0, g, dilation,
                                            k_buf.reshape(db, length * ATT_SLOTS, ATT_HEAD_DIM),
                                            v_buf.reshape(db, length * ATT_SLOTS, ATT_HEAD_DIM))
        outs_s.append(o.reshape(n_sample, ATT_OUT_WIDTH))
        lses_s.append(lse.reshape(n_sample, ATT_OUT_WIDTH))
        new_kv_s += [k_new.reshape(1, db, length, ATT_SLOTS, ATT_HEAD_DIM),
                     v_new.reshape(1, db, length, ATT_SLOTS, ATT_HEAD_DIM)]
    att = jnp.concatenate([_combine(outs_p, lses_p), _combine(outs_s, lses_s)], axis=0)

    hg_p, state_p = _hgrn_prompt(proj, seq, hg_lower_bound, hg_norm[0], hg_width)
    hg_s, state_s = _hgrn_sample(proj3, batch0, state_hgrn[0], hg_lower_bound, hg_norm[0])
    hg = jnp.concatenate([hg_p, hg_s.reshape(n_sample, hg_width)], axis=0)

    merged = _merge_proj(att, hg, w_proj_att[0].astype(BF16), w_proj_hg[0].astype(BF16), gates)
    mixed = _matmul(merged, w_out[0].astype(BF16), name="out_proj")
    x1, h2 = _resid_norm(xp, xs, mixed, norm_mix_post[0], norm_ffn_pre[0])

    u = _matmul(h2, w_up[0].astype(BF16), out_dtype=BF16, act="relu2", name="ffn_up")
    z = _matmul_ksplit(u, w_down[0].astype(BF16), tk=4096, name="ffn_down")
    y_prompt, y_sample = _resid_final(x1, z, norm_ffn_post[0], n_prompt)
    y_prompt = y_prompt.reshape(batch, seq, d_model)
    y_sample = y_sample.reshape(db, t_new, d_model)
    heads = hg_width // HG_EXPAND
    return (y_prompt, y_sample, *new_kv_p,
            state_p.reshape(1, batch, heads, HG_EXPAND, HG_HEAD_V),
            *new_kv_s, state_s[None])
```

```python
import functools
import math

import jax
import jax.numpy as jnp
from jax import lax
from jax.experimental import pallas as pl
from jax.experimental.pallas import tpu as pltpu

ATT_HEAD_DIM = 128
ATT_SLOTS = 8
ATT_GROUPS = ((128, 1), (512, 4), (2048, 16))
ATT_BLOCK = 128
ATT_OUT_WIDTH = ATT_SLOTS * ATT_HEAD_DIM
ATT_WIDTH = len(ATT_GROUPS) * ATT_OUT_WIDTH
ATT_SCALE = ATT_HEAD_DIM ** -0.5
HG_EXPAND = 128
HG_HEAD_V = 128
HG_SCALE = HG_EXPAND ** -0.5
HG_CHUNK = 64
HG_SUB = 8
RMS_EPS = 1e-6
NEG_BIG = -1e30

V7X_VMEM_BYTES = 64 * 1024 * 1024
VMEM_LIMIT = V7X_VMEM_BYTES - 8 * 1024 * 1024
LANES = 128
SUBLANES = 8

BF16 = jnp.bfloat16
F32 = jnp.float32


def _params(*sem):
    return pltpu.CompilerParams(dimension_semantics=sem, vmem_limit_bytes=VMEM_LIMIT)


def _tile(n, target, mult):
    if n <= target:
        return n
    t = (target // mult) * mult
    while t >= mult:
        if n % t == 0:
            return t
        t -= mult
    raise ValueError(f"no tile for {n}")


def _sigmoid(x):
    return 1.0 / (1.0 + jnp.exp(-x))


def _dot(a, b):
    return jnp.dot(a, b, preferred_element_type=F32)


def _dot_nt(a, b):
    return lax.dot_general(a, b, (((1,), (1,)), ((), ())), preferred_element_type=F32)


def _dot_tn(a, b):
    return lax.dot_general(a, b, (((0,), (0,)), ((), ())), preferred_element_type=F32)


def _rms(x, gain):
    return x * lax.rsqrt(jnp.mean(x * x, axis=-1, keepdims=True) + RMS_EPS) * gain


def _split_rows(n_prompt, n_sample, d, target):
    tm = _tile(math.gcd(n_prompt, n_sample), target, SUBLANES)
    np_tiles = n_prompt // tm
    prompt = pl.BlockSpec((tm, d), lambda i: (jnp.minimum(i, np_tiles - 1), 0))
    sample = pl.BlockSpec((tm, d), lambda i: (jnp.maximum(i - np_tiles, 0), 0))
    stacked = pl.BlockSpec((tm, d), lambda i: (i, 0))
    vec = pl.BlockSpec((1, d), lambda i: (0, 0))
    return tm, np_tiles, prompt, sample, stacked, vec


def _norm_cast_kernel(xp_ref, xs_ref, g_ref, o_ref, *, np_tiles):
    i = pl.program_id(0)

    @pl.when(i < np_tiles)
    def _():
        o_ref[...] = _rms(xp_ref[...], g_ref[...]).astype(o_ref.dtype)

    @pl.when(i >= np_tiles)
    def _():
        o_ref[...] = _rms(xs_ref[...], g_ref[...]).astype(o_ref.dtype)


def _norm_cast(xp, xs, gain):
    d = xp.shape[1]
    m = xp.shape[0] + xs.shape[0]
    tm, np_tiles, prompt, sample, stacked, vec = _split_rows(xp.shape[0], xs.shape[0], d, 256)
    return pl.pallas_call(
        functools.partial(_norm_cast_kernel, np_tiles=np_tiles),
        grid=(m // tm,),
        in_specs=[prompt, sample, vec],
        out_specs=stacked,
        out_shape=jax.ShapeDtypeStruct((m, d), BF16),
        compiler_params=_params("arbitrary"),
        name="norm_cast",
    )(xp, xs, gain.reshape(1, d))


def _resid_norm_kernel(xp_ref, xs_ref, y_ref, gpost_ref, gpre_ref, x1_ref, h_ref, *, np_tiles):
    i = pl.program_id(0)

    def body(x_ref):
        x1 = x_ref[...] + _rms(y_ref[...], gpost_ref[...])
        x1_ref[...] = x1
        h_ref[...] = _rms(x1, gpre_ref[...]).astype(h_ref.dtype)

    pl.when(i < np_tiles)(lambda: body(xp_ref))
    pl.when(i >= np_tiles)(lambda: body(xs_ref))


def _resid_norm(xp, xs, y, gain_post, gain_pre):
    m, d = y.shape
    tm, np_tiles, prompt, sample, stacked, vec = _split_rows(xp.shape[0], xs.shape[0], d, 256)
    return pl.pallas_call(
        functools.partial(_resid_norm_kernel, np_tiles=np_tiles),
        grid=(m // tm,),
        in_specs=[prompt, sample, stacked, vec, vec],
        out_specs=[stacked, stacked],
        out_shape=[jax.ShapeDtypeStruct((m, d), F32), jax.ShapeDtypeStruct((m, d), BF16)],
        compiler_params=_params("arbitrary"),
        name="resid_norm",
    )(xp, xs, y, gain_post.reshape(1, d), gain_pre.reshape(1, d))


def _resid_final_kernel(x_ref, y_ref, g_ref, op_ref, os_ref, *, np_tiles):
    i = pl.program_id(0)

    @pl.when(i < np_tiles)
    def _():
        op_ref[...] = x_ref[...] + _rms(y_ref[...], g_ref[...])

    @pl.when(i >= np_tiles)
    def _():
        os_ref[...] = x_ref[...] + _rms(y_ref[...], g_ref[...])


def _resid_final(x, y, gain, n_prompt):
    m, d = x.shape
    tm, np_tiles, prompt, sample, stacked, vec = _split_rows(n_prompt, m - n_prompt, d, 256)
    return pl.pallas_call(
        functools.partial(_resid_final_kernel, np_tiles=np_tiles),
        grid=(m // tm,),
        in_specs=[stacked, stacked, vec],
        out_specs=[prompt, sample],
        out_shape=[jax.ShapeDtypeStruct((n_prompt, d), F32),
                   jax.ShapeDtypeStruct((m - n_prompt, d), F32)],
        compiler_params=_params("arbitrary"),
        name="resid_final",
    )(x, y, gain.reshape(1, d))


def _mm_kernel(a_ref, w_ref, o_ref, *, act):
    acc = _dot(a_ref[...], w_ref[...])
    if act == "relu2":
        acc = jnp.maximum(acc, 0.0)
        acc = acc * acc
    o_ref[...] = acc.astype(o_ref.dtype)


def _mm_bias_sigmoid_kernel(a_ref, w_ref, b_ref, o_ref):
    o_ref[...] = _sigmoid(_dot(a_ref[...], w_ref[...]) + b_ref[...]).astype(o_ref.dtype)


def _matmul(a, w, *, out_dtype=F32, act=None, bias=None, name="matmul"):
    m, k = a.shape
    n = w.shape[1]
    tm = _tile(m, 1024, SUBLANES)
    tn = _tile(n, 1024, LANES)
    a_spec = pl.BlockSpec((tm, k), lambda i, j: (i, 0))
    w_spec = pl.BlockSpec((k, tn), lambda i, j: (0, j))
    o_spec = pl.BlockSpec((tm, tn), lambda i, j: (i, j))
    if bias is None:
        body, ins, specs = functools.partial(_mm_kernel, act=act), (a, w), [a_spec, w_spec]
    else:
        body, ins = _mm_bias_sigmoid_kernel, (a, w, bias.reshape(1, n))
        specs = [a_spec, w_spec, pl.BlockSpec((1, tn), lambda i, j: (0, j))]
    return pl.pallas_call(
        body,
        grid=(m // tm, n // tn),
        in_specs=specs,
        out_specs=o_spec,
        out_shape=jax.ShapeDtypeStruct((m, n), out_dtype),
        compiler_params=_params("parallel", "arbitrary"),
        name=name,
    )(*ins)


def _mm_acc_kernel(a_ref, w_ref, o_ref):
    @pl.when(pl.program_id(2) == 0)
    def _():
        o_ref[...] = jnp.zeros_like(o_ref)

    o_ref[...] += _dot(a_ref[...], w_ref[...])


def _matmul_ksplit(a, w, *, tk, name):
    m, k = a.shape
    n = w.shape[1]
    tm = _tile(m, 1024, SUBLANES)
    tn = _tile(n, 1024, LANES)
    tk = _tile(k, tk, LANES)
    return pl.pallas_call(
        _mm_acc_kernel,
        grid=(m // tm, n // tn, k // tk),
        in_specs=[pl.BlockSpec((tm, tk), lambda i, j, l: (i, l)),
                  pl.BlockSpec((tk, tn), lambda i, j, l: (l, j))],
        out_specs=pl.BlockSpec((tm, tn), lambda i, j, l: (i, j)),
        out_shape=jax.ShapeDtypeStruct((m, n), F32),
        compiler_params=_params("parallel", "arbitrary", "arbitrary"),
        name=name,
    )(a, w)


def _merge_kernel(att_ref, hg_ref, wa_ref, wh_ref, ga_ref, gh_ref, o_ref):
    pa = _dot(att_ref[...], wa_ref[...])
    ph = _dot(hg_ref[...], wh_ref[...])
    o_ref[...] = (ga_ref[...] * pa + gh_ref[...] * ph).astype(o_ref.dtype)


def _merge_proj(att, hg, w_att, w_hg, gates):
    m, ka = att.shape
    kh = hg.shape[1]
    d = w_att.shape[1]
    tm = _tile(m, 1024, SUBLANES)
    tn = _tile(d, 1024, LANES)
    nj = d // tn
    return pl.pallas_call(
        _merge_kernel,
        grid=(m // tm, nj),
        in_specs=[pl.BlockSpec((tm, ka), lambda i, j: (i, 0)),
                  pl.BlockSpec((tm, kh), lambda i, j: (i, 0)),
                  pl.BlockSpec((ka, tn), lambda i, j: (0, j)),
                  pl.BlockSpec((kh, tn), lambda i, j: (0, j)),
                  pl.BlockSpec((tm, tn), lambda i, j: (i, j)),
                  pl.BlockSpec((tm, tn), lambda i, j: (i, j + nj))],
        out_specs=pl.BlockSpec((tm, tn), lambda i, j: (i, j)),
        out_shape=jax.ShapeDtypeStruct((m, d), BF16),
        compiler_params=_params("parallel", "arbitrary"),
        name="merge_proj",
    )(att, hg, w_att, w_hg, gates, gates)


def _attn_prompt_kernel(q_ref, kp_ref, kc_ref, vp_ref, vc_ref, o_ref, lse_ref, *, dilation):
    jb = pl.program_id(1)
    blk = ATT_BLOCK
    r = lax.broadcasted_iota(jnp.int32, (blk, 2 * blk), 0)
    c = lax.broadcasted_iota(jnp.int32, (blk, 2 * blk), 1)
    dist = blk + r - c
    mask = (dist >= 0) & (dist <= blk) & ((c >= blk) | (jb > 0))
    for res in range(dilation):
        rows = pl.ds(res, blk, stride=dilation) if dilation > 1 else slice(None)
        for h in range(q_ref.shape[1] // ATT_HEAD_DIM):
            sl = slice(h * ATT_HEAD_DIM, (h + 1) * ATT_HEAD_DIM)
            q = q_ref[rows, sl].astype(BF16)
            k = jnp.concatenate([kp_ref[rows, sl], kc_ref[rows, sl]], axis=0).astype(BF16)
            v = jnp.concatenate([vp_ref[rows, sl], vc_ref[rows, sl]], axis=0).astype(BF16)
            s = jnp.where(mask, _dot_nt(q, k) * ATT_SCALE, NEG_BIG)
            m = jnp.max(s, axis=-1, keepdims=True)
            p = jnp.exp(s - m)
            l = jnp.sum(p, axis=-1, keepdims=True)
            o_ref[rows, sl] = _dot(p.astype(BF16), v) / l
            lse_ref[rows, sl] = jnp.broadcast_to(m + jnp.log(l), (blk, ATT_HEAD_DIM))


def _attn_prompt(proj, seq, group, dilation):
    chunk = ATT_BLOCK * dilation
    nch = seq // chunk
    lw = ATT_OUT_WIDTH if dilation == 1 else ATT_HEAD_DIM
    nhb = ATT_OUT_WIDTH // lw
    ngroups = len(ATT_GROUPS)

    def spec(col, prev):
        c0 = col * nhb
        if prev:
            return pl.BlockSpec((chunk, lw), lambda hb, j: (jnp.maximum(j - 1, 0), c0 + hb))
        return pl.BlockSpec((chunk, lw), lambda hb, j: (j, c0 + hb))

    out_spec = pl.BlockSpec((chunk, lw), lambda hb, j: (j, hb))
    shape = jax.ShapeDtypeStruct((seq, ATT_OUT_WIDTH), F32)
    return pl.pallas_call(
        functools.partial(_attn_prompt_kernel, dilation=dilation),
        grid=(nhb, nch),
        in_specs=[spec(group, False),
                  spec(ngroups + group, True), spec(ngroups + group, False),
                  spec(2 * ngroups + group, True), spec(2 * ngroups + group, False)],
        out_specs=[out_spec, out_spec],
        out_shape=[shape, shape],
        compiler_params=_params("parallel", "arbitrary"),
        name=f"attn_prompt_d{dilation}",
    )(proj, proj, proj, proj, proj)


def _attn_sample_kernel(q_ref, kn_ref, vn_ref, kn2_ref, vn2_ref, kb_ref, vb_ref, kx_ref, vx_ref,
                        o_ref, lse_ref, ko_ref, vo_ref, m_sc, l_sc, acc_sc, *, dilation, t_new):
    lt = pl.program_id(1)
    nlt = pl.num_programs(1)
    nh = ATT_SLOTS
    tl = kb_ref.shape[1] // nh
    shift = t_new * nh
    dmask = dilation - 1
    rows = nh * t_new

    def lanes(h):
        return slice(h * ATT_HEAD_DIM, (h + 1) * ATT_HEAD_DIM)

    def per_head(fn):
        return jnp.concatenate([fn(h) for h in range(nh)], axis=0)

    def head_rows(x, h):
        return x[h * t_new:(h + 1) * t_new]

    q16 = [q_ref[0, :, lanes(h)].astype(BF16) for h in range(nh)]

    @pl.when(lt == 0)
    def _():
        s = per_head(lambda h: _dot_nt(q16[h], kn_ref[0, :, lanes(h)].astype(BF16))) * ATT_SCALE
        t = lax.broadcasted_iota(jnp.int32, (rows, t_new), 0) % t_new
        i = lax.broadcasted_iota(jnp.int32, (rows, t_new), 1)
        s = jnp.where((i <= t) & (((t - i) & dmask) == 0), s, NEG_BIG)
        m = jnp.max(s, axis=-1, keepdims=True)
        p = jnp.exp(s - m)
        m_sc[...] = m
        l_sc[...] = jnp.sum(p, axis=-1, keepdims=True)
        acc_sc[...] = per_head(lambda h: _dot(head_rows(p, h).astype(BF16),
                                              vn_ref[0, :, lanes(h)].astype(BF16)))

    s = per_head(lambda h: _dot_nt(q16[h], kb_ref[0, pl.ds(h, tl, stride=nh), :].astype(BF16))) * ATT_SCALE
    t = lax.broadcasted_iota(jnp.int32, (rows, tl), 0) % t_new
    j = lax.broadcasted_iota(jnp.int32, (rows, tl), 1) + lt * tl
    s = jnp.where((j >= t) & (((j - t) & dmask) == 0), s, NEG_BIG)
    m_old = m_sc[...]
    m_new = jnp.maximum(m_old, jnp.max(s, axis=-1, keepdims=True))
    alpha = jnp.exp(m_old - m_new)
    p = jnp.exp(s - m_new)
    l_sc[...] = alpha * l_sc[...] + jnp.sum(p, axis=-1, keepdims=True)
    pv = per_head(lambda h: _dot(head_rows(p, h).astype(BF16),
                                 vb_ref[0, pl.ds(h, tl, stride=nh), :].astype(BF16)))
    acc_sc[...] = alpha * acc_sc[...] + pv
    m_sc[...] = m_new

    keep = tl * nh - shift
    ko_ref[0, :keep] = kb_ref[0, shift:]
    vo_ref[0, :keep] = vb_ref[0, shift:]

    @pl.when(lt < nlt - 1)
    def _():
        ko_ref[0, keep:] = kx_ref[0]
        vo_ref[0, keep:] = vx_ref[0]

    @pl.when(lt == nlt - 1)
    def _():
        ko_ref[0, keep:] = kn2_ref[0]
        vo_ref[0, keep:] = vn2_ref[0]
        l = l_sc[...]
        out = acc_sc[...] / l
        lse = m_sc[...] + jnp.log(l)
        for h in range(nh):
            o_ref[0, :, lanes(h)] = head_rows(out, h)
            lse_ref[0, :, lanes(h)] = jnp.broadcast_to(head_rows(lse, h), (t_new, ATT_HEAD_DIM))


def _attn_sample(proj3, batch0, group, dilation, k_buf, v_buf):
    db, rows_total, _ = k_buf.shape
    nh = ATT_SLOTS
    length = rows_total // nh
    t_new = proj3.shape[1]
    ngroups = len(ATT_GROUPS)
    tl = _tile(length, 1024, SUBLANES)
    nlt = length // tl
    step = tl // t_new
    last = length // t_new - 1
    shift = t_new * nh

    def new_rows(col):
        c0 = col * ATT_OUT_WIDTH
        return proj3[batch0:, :, c0:c0 + ATT_OUT_WIDTH].reshape(db, shift, ATT_HEAD_DIM)

    def new_spec(col):
        return pl.BlockSpec((1, t_new, ATT_OUT_WIDTH), lambda b, l: (batch0 + b, 0, col))

    new2_spec = pl.BlockSpec((1, shift, ATT_HEAD_DIM), lambda b, l: (b, 0, 0))
    buf_spec = pl.BlockSpec((1, tl * nh, ATT_HEAD_DIM), lambda b, l: (b, l, 0))
    next_spec = pl.BlockSpec((1, shift, ATT_HEAD_DIM),
                             lambda b, l: (b, jnp.minimum((l + 1) * step, last), 0))
    small = pl.BlockSpec((1, t_new, ATT_OUT_WIDTH), lambda b, l: (b, 0, 0))
    small_shape = jax.ShapeDtypeStruct((db, t_new, ATT_OUT_WIDTH), F32)
    buf_shape = jax.ShapeDtypeStruct(k_buf.shape, F32)
    rows = nh * t_new
    return pl.pallas_call(
        functools.partial(_attn_sample_kernel, dilation=dilation, t_new=t_new),
        grid=(db, nlt),
        in_specs=[new_spec(group), new_spec(ngroups + group), new_spec(2 * ngroups + group),
                  new2_spec, new2_spec, buf_spec, buf_spec, next_spec, next_spec],
        out_specs=[small, small, buf_spec, buf_spec],
        out_shape=[small_shape, small_shape, buf_shape, buf_shape],
        scratch_shapes=[pltpu.VMEM((rows, 1), F32), pltpu.VMEM((rows, 1), F32),
                        pltpu.VMEM((rows, ATT_HEAD_DIM), F32)],
        compiler_params=_params("parallel", "arbitrary"),
        name=f"attn_sample_d{dilation}",
    )(proj3, proj3, proj3, new_rows(ngroups + group), new_rows(2 * ngroups + group),
      k_buf, v_buf, k_buf, v_buf)


def _combine_kernel(o0, o1, o2, l0, l1, l2, out_ref):
    a, b, c = l0[...], l1[...], l2[...]
    m = jnp.maximum(jnp.maximum(a, b), c)
    ea, eb, ec = jnp.exp(a - m), jnp.exp(b - m), jnp.exp(c - m)
    num = ea * o0[...] + eb * o1[...] + ec * o2[...]
    out_ref[...] = (num / (ea + eb + ec)).astype(out_ref.dtype)


def _combine(outs, lses):
    m, w = outs[0].shape
    tm = _tile(m, 512, SUBLANES)
    spec = pl.BlockSpec((tm, w), lambda i: (i, 0))
    return pl.pallas_call(
        _combine_kernel,
        grid=(m // tm,),
        in_specs=[spec] * 6,
        out_specs=spec,
        out_shape=jax.ShapeDtypeStruct((m, w), BF16),
        compiler_params=_params("parallel"),
        name="attn_combine",
    )(*outs, *lses)


def _cumsum_rows(x):
    c = x.shape[0]
    r = lax.broadcasted_iota(jnp.int32, (c, c), 0)
    s = lax.broadcasted_iota(jnp.int32, (c, c), 1)
    tri = jnp.where(r >= s, 1.0, 0.0).astype(BF16)
    hi = x.astype(BF16)
    rem = x - hi.astype(F32)
    mid = rem.astype(BF16)
    lo = (rem - mid.astype(F32)).astype(BF16)
    return _dot(tri, hi) + _dot(tri, mid) + _dot(tri, lo)


def _col_bcast(w):
    hi = w.astype(BF16).astype(F32)
    mid = (w - hi).astype(BF16).astype(F32)
    lo = (w - hi - mid).astype(BF16).astype(F32)
    terms = jnp.concatenate([hi, mid, lo, jnp.zeros((SUBLANES - 3, LANES), F32)], axis=0).astype(BF16)
    ones = jnp.where(lax.broadcasted_iota(jnp.int32, (SUBLANES, LANES), 0) < 3, 1.0, 0.0).astype(BF16)
    return _dot_tn(terms, ones)


def _lower_bound(lb_ref):
    a = lb_ref[...]
    e = jnp.exp(a - jnp.max(a, axis=0, keepdims=True))
    return e[0:1] / jnp.sum(e, axis=0, keepdims=True)


def _hgrn_chunk(qg, fg, ig, gg, lb, gain, st):
    c = qg.shape[0]
    q = qg * _sigmoid(qg) * HG_SCALE
    forget = lb + (1.0 - lb) * _sigmoid(fg)
    k = 1.0 - forget
    v = ig
    b = _cumsum_rows(jnp.log(forget))
    v16 = v.astype(BF16)

    nsub = c // HG_SUB
    b3 = b.reshape(nsub, HG_SUB, LANES)
    q3 = q.reshape(nsub, HG_SUB, LANES)
    k3 = k.reshape(nsub, HG_SUB, LANES)
    pos = lax.broadcasted_iota(jnp.int32, (nsub, HG_SUB, LANES), 1)
    row = lax.broadcasted_iota(jnp.int32, (c, c), 0)
    col = lax.broadcasted_iota(jnp.int32, (c, c), 1)
    scores = jnp.zeros((c, c), F32)
    for s in range(HG_SUB):
        diff = jnp.where(pos >= s, b3 - b3[:, s:s + 1, :], NEG_BIG)
        w = jnp.sum(q3 * k3[:, s:s + 1, :] * jnp.exp(diff), axis=-1, keepdims=True)
        w = jnp.broadcast_to(w.reshape(c, 1), (c, c))
        scores = jnp.where(col == (row // HG_SUB) * HG_SUB + s, w, scores)

    width = HG_SUB
    while width < c:
        pair = 2 * width
        bm = jnp.concatenate(
            [jnp.broadcast_to(b[p * pair + width - 1:p * pair + width], (pair, LANES))
             for p in range(c // pair)], axis=0)
        e = jnp.exp(-jnp.abs(b - bm))
        right = (lax.broadcasted_iota(jnp.int32, (c, LANES), 0) // width) % 2 == 1
        ql = jnp.where(right, q * e, 0.0).astype(BF16)
        kl = jnp.where(right, 0.0, k * e).astype(BF16)
        a = _dot_nt(ql, kl)
        scores = jnp.where((row // pair == col // pair) & (row // width != col // width) & (row > col),
                           a, scores)
        width = pair

    o = _dot(scores.astype(BF16), v16) + _dot_nt((q * jnp.exp(b)).astype(BF16), st.astype(BF16))
    b_last = b[c - 1:c]
    st_new = st * jnp.exp(b_last) + _dot_tn(v16, (k * jnp.exp(b_last - b)).astype(BF16))
    out = _rms(o, gain) * (gg * _sigmoid(gg))
    return out, st_new


def _hgrn_prompt_kernel(q_ref, f_ref, i_ref, g_ref, lb_ref, gain_ref, o_ref, s_ref, st_sc):
    tb = pl.program_id(1)

    @pl.when(tb == 0)
    def _():
        st_sc[...] = jnp.zeros_like(st_sc)

    nheads = st_sc.shape[0]
    gain = gain_ref[...]
    lbs = [_lower_bound(lb_ref.at[:, h * LANES:(h + 1) * LANES]) for h in range(nheads)]
    sts = [st_sc[h] for h in range(nheads)]
    for ci in range(q_ref.shape[0] // HG_CHUNK):
        rs = slice(ci * HG_CHUNK, (ci + 1) * HG_CHUNK)
        for h in range(nheads):
            sl = slice(h * LANES, (h + 1) * LANES)
            out, sts[h] = _hgrn_chunk(q_ref[rs, sl], f_ref[rs, sl], i_ref[rs, sl], g_ref[rs, sl],
                                      lbs[h], gain, sts[h])
            o_ref[rs, sl] = out.astype(o_ref.dtype)
    for h in range(nheads):
        st_sc[h] = sts[h]

    @pl.when(tb == pl.num_programs(1) - 1)
    def _():
        for h in range(nheads):
            s_ref[h] = sts[h].T


def _hgrn_prompt(proj, seq, hg_lower_bound, hg_norm, hg_width):
    heads = hg_width // HG_EXPAND
    hw = HG_EXPAND
    assert 3 * ATT_WIDTH % hw == 0
    base = 3 * ATT_WIDTH // hw
    per = hg_width // hw
    hpb = hw // HG_EXPAND
    tb = _tile(seq, 512, HG_CHUNK)

    def col(which):
        return pl.BlockSpec((tb, hw), lambda g, t: (t, base + which * per + g))

    depth1 = hg_lower_bound.shape[0]
    return pl.pallas_call(
        _hgrn_prompt_kernel,
        grid=(per, seq // tb),
        in_specs=[col(0), col(1), col(2), col(3),
                  pl.BlockSpec((depth1, hw), lambda g, t: (0, g)),
                  pl.BlockSpec((1, LANES), lambda g, t: (0, 0))],
        out_specs=[pl.BlockSpec((tb, hw), lambda g, t: (t, g)),
                   pl.BlockSpec((hpb, HG_EXPAND, HG_HEAD_V), lambda g, t: (g, 0, 0))],
        out_shape=[jax.ShapeDtypeStruct((seq, hg_width), BF16),
                   jax.ShapeDtypeStruct((heads, HG_EXPAND, HG_HEAD_V), F32)],
        scratch_shapes=[pltpu.VMEM((hpb, HG_EXPAND, HG_HEAD_V), F32)],
        compiler_params=_params("parallel", "arbitrary"),
        name="hgrn_prompt",
    )(proj, proj, proj, proj, hg_lower_bound, hg_norm.reshape(1, LANES))


def _hgrn_sample_kernel(q_ref, f_ref, i_ref, g_ref, lb_ref, gain_ref, s_ref, o_ref, so_ref):
    nheads = s_ref.shape[1]
    c = q_ref.shape[1]
    qg, fg, v, gg = q_ref[0], f_ref[0], i_ref[0], g_ref[0]
    lb = _lower_bound(lb_ref)
    gain = gain_ref[...]
    q = qg * _sigmoid(qg) * HG_SCALE
    forget = lb + (1.0 - lb) * _sigmoid(fg)
    k = 1.0 - forget
    b = _cumsum_rows(jnp.log(forget))
    b_last = b[c - 1:c]
    qdec = (q * jnp.exp(b)).astype(BF16)
    kdec = (k * jnp.exp(b_last - b)).astype(BF16)
    decay = jnp.exp(b_last)
    v16 = v.astype(BF16)
    t = lax.broadcasted_iota(jnp.int32, b.shape, 0)
    pair = [q * k[s:s + 1] * jnp.exp(jnp.where(t >= s, b - b[s:s + 1], NEG_BIG)) for s in range(c)]
    outs = []
    for h in range(nheads):
        sl = slice(h * LANES, (h + 1) * LANES)
        st = s_ref[0, h]
        o = _dot(qdec[:, sl], st.astype(BF16))
        for s in range(c):
            o = o + jnp.sum(pair[s][:, sl], axis=-1, keepdims=True) * v[s:s + 1, sl]
        so_ref[0, h] = st * _col_bcast(decay[:, sl]) + _dot_tn(kdec[:, sl], v16[:, sl])
        outs.append(_rms(o, gain))
    o_ref[0] = (jnp.concatenate(outs, axis=1) * (gg * _sigmoid(gg))).astype(o_ref.dtype)


def _hgrn_sample(proj3, batch0, state, hg_lower_bound, hg_norm):
    db, heads = state.shape[:2]
    t_new = proj3.shape[1]
    hg_width = heads * HG_EXPAND
    cw = _tile(hg_width, ATT_OUT_WIDTH, LANES)
    assert 3 * ATT_WIDTH % cw == 0
    base = 3 * ATT_WIDTH // cw
    per = hg_width // cw
    hb = cw // HG_EXPAND

    def col(which):
        return pl.BlockSpec((1, t_new, cw), lambda b, c: (batch0 + b, 0, base + which * per + c))

    depth1 = hg_lower_bound.shape[0]
    st_spec = pl.BlockSpec((1, hb, HG_EXPAND, HG_HEAD_V), lambda b, c: (b, c, 0, 0))
    return pl.pallas_call(
        _hgrn_sample_kernel,
        grid=(db, per),
        in_specs=[col(0), col(1), col(2), col(3),
                  pl.BlockSpec((depth1, cw), lambda b, c: (0, c)),
                  pl.BlockSpec((1, LANES), lambda b, c: (0, 0)),
                  st_spec],
        out_specs=[pl.BlockSpec((1, t_new, cw), lambda b, c: (b, 0, c)), st_spec],
        out_shape=[jax.ShapeDtypeStruct((db, t_new, hg_width), BF16),
                   jax.ShapeDtypeStruct(state.shape, F32)],
        compiler_params=_params("parallel", "parallel"),
        name="hgrn_sample",
    )(proj3, proj3, proj3, proj3, hg_lower_bound, hg_norm.reshape(1, LANES), state)


def kernel(x_prompt, x_sample, cache_k_w128, cache_v_w128, cache_k_w512, cache_v_w512,
           cache_k_w2048, cache_v_w2048, state_hgrn, hg_lower_bound, w_in, w_gate, b_gate,
           w_proj_att, w_proj_hg, w_out, hg_norm, norm_mix_pre, norm_mix_post,
           norm_ffn_pre, norm_ffn_post, w_up, w_down):
    assert w_in.shape[0] == 1, "single-layer trunk"
    batch, seq, d_model = x_prompt.shape
    db, t_new, _ = x_sample.shape
    assert batch == 1
    hg_width = w_proj_hg.shape[1]
    n_prompt = batch * seq
    n_sample = db * t_new
    caches = (cache_k_w128, cache_v_w128, cache_k_w512, cache_v_w512, cache_k_w2048, cache_v_w2048)

    xp = x_prompt.reshape(n_prompt, d_model)
    xs = x_sample.reshape(n_sample, d_model)
    h = _norm_cast(xp, xs, norm_mix_pre[0])
    proj = _matmul(h, w_in[0].astype(BF16), name="in_proj")
    gates = _matmul(h, w_gate[0].astype(BF16), bias=b_gate[0], name="gate_proj")
    in_width = proj.shape[1]
    proj3 = proj.reshape((n_prompt + n_sample) // t_new, t_new, in_width)
    batch0 = n_prompt // t_new

    outs_p, lses_p, outs_s, lses_s, new_kv_p, new_kv_s = [], [], [], [], [], []
    for g, (window, dilation) in enumerate(ATT_GROUPS):
        o, lse = _attn_prompt(proj, seq, g, dilation)
        outs_p.append(o)
        lses_p.append(lse)
        keep = min(window, seq)
        for part in (1, 2):
            c0 = part * ATT_WIDTH + g * ATT_OUT_WIDTH
            rows = proj[n_prompt - keep:n_prompt, c0:c0 + ATT_OUT_WIDTH]
            new_kv_p.append(rows.reshape(1, batch, keep, ATT_SLOTS, ATT_HEAD_DIM))
        k_buf, v_buf = caches[2 * g][0], caches[2 * g + 1][0]
        length = k_buf.shape[1]
        assert length == window and length == dilation * ATT_BLOCK
        o, lse, k_new, v_new = _attn_sample(proj3, batch0, g, dilation,
                                            k_buf.reshape(db, length * ATT_SLOTS, ATT_HEAD_DIM),
                                            v_buf.reshape(db, length * ATT_SLOTS, ATT_HEAD_DIM))
        outs_s.append(o.reshape(n_sample, ATT_OUT_WIDTH))
        lses_s.append(lse.reshape(n_sample, ATT_OUT_WIDTH))
        new_kv_s += [k_new.reshape(1, db, length, ATT_SLOTS, ATT_HEAD_DIM),
                     v_new.reshape(1, db, length, ATT_SLOTS, ATT_HEAD_DIM)]
    att = jnp.concatenate([_combine(outs_p, lses_p), _combine(outs_s, lses_s)], axis=0)

    hg_p, state_p = _hgrn_prompt(proj, seq, hg_lower_bound, hg_norm[0], hg_width)
    hg_s, state_s = _hgrn_sample(proj3, batch0, state_hgrn[0], hg_lower_bound, hg_norm[0])
    hg = jnp.concatenate([hg_p, hg_s.reshape(n_sample, hg_width)], axis=0)

    merged = _merge_proj(att, hg, w_proj_att[0].astype(BF16), w_proj_hg[0].astype(BF16), gates)
    mixed = _matmul(merged, w_out[0].astype(BF16), name="out_proj")
    x1, h2 = _resid_norm(xp, xs, mixed, norm_mix_post[0], norm_ffn_pre[0])

    u = _matmul(h2, w_up[0].astype(BF16), out_dtype=BF16, act="relu2", name="ffn_up")
    z = _matmul_ksplit(u, w_down[0].astype(BF16), tk=4096, name="ffn_down")
    y_prompt, y_sample = _resid_final(x1, z, norm_ffn_post[0], n_prompt)
    y_prompt = y_prompt.reshape(batch, seq, d_model)
    y_sample = y_sample.reshape(db, t_new, d_model)
    heads = hg_width // HG_EXPAND
    return (y_prompt, y_sample, *new_kv_p,
            state_p.reshape(1, batch, heads, HG_EXPAND, HG_HEAD_V),
            *new_kv_s, state_s[None])
```

```python
import functools
import math

import jax
import jax.numpy as jnp
from jax import lax
from jax.experimental import pallas as pl
from jax.experimental.pallas import tpu as pltpu

ATT_HEAD_DIM = 128
ATT_SLOTS = 8
ATT_GROUPS = ((128, 1), (512, 4), (2048, 16))
ATT_BLOCK = 128
ATT_OUT_WIDTH = ATT_SLOTS * ATT_HEAD_DIM
ATT_WIDTH = len(ATT_GROUPS) * ATT_OUT_WIDTH
ATT_SCALE = ATT_HEAD_DIM ** -0.5
HG_EXPAND = 128
HG_HEAD_V = 128
HG_SCALE = HG_EXPAND ** -0.5
HG_CHUNK = 64
HG_SUB = 8
RMS_EPS = 1e-6
NEG_BIG = -1e30

V7X_VMEM_BYTES = 64 * 1024 * 1024
VMEM_LIMIT = V7X_VMEM_BYTES - 8 * 1024 * 1024
LANES = 128
SUBLANES = 8

BF16 = jnp.bfloat16
F32 = jnp.float32


def _params(*sem):
    return pltpu.CompilerParams(dimension_semantics=sem, vmem_limit_bytes=VMEM_LIMIT)


def _tile(n, target, mult):
    if n <= target:
        return n
    t = (target // mult) * mult
    while t >= mult:
        if n % t == 0:
            return t
        t -= mult
    raise ValueError(f"no tile for {n}")


def _sigmoid(x):
    return 1.0 / (1.0 + jnp.exp(-x))


def _dot(a, b):
    return jnp.dot(a, b, preferred_element_type=F32)


def _dot_nt(a, b):
    return lax.dot_general(a, b, (((1,), (1,)), ((), ())), preferred_element_type=F32)


def _dot_tn(a, b):
    return lax.dot_general(a, b, (((0,), (0,)), ((), ())), preferred_element_type=F32)


def _rms(x, gain):
    return x * lax.rsqrt(jnp.mean(x * x, axis=-1, keepdims=True) + RMS_EPS) * gain


def _split_rows(n_prompt, n_sample, d, target):
    tm = _tile(math.gcd(n_prompt, n_sample), target, SUBLANES)
    np_tiles = n_prompt // tm
    prompt = pl.BlockSpec((tm, d), lambda i: (jnp.minimum(i, np_tiles - 1), 0))
    sample = pl.BlockSpec((tm, d), lambda i: (jnp.maximum(i - np_tiles, 0), 0))
    stacked = pl.BlockSpec((tm, d), lambda i: (i, 0))
    vec = pl.BlockSpec((1, d), lambda i: (0, 0))
    return tm, np_tiles, prompt, sample, stacked, vec


def _norm_cast_kernel(xp_ref, xs_ref, g_ref, o_ref, *, np_tiles):
    i = pl.program_id(0)

    @pl.when(i < np_tiles)
    def _():
        o_ref[...] = _rms(xp_ref[...], g_ref[...]).astype(o_ref.dtype)

    @pl.when(i >= np_tiles)
    def _():
        o_ref[...] = _rms(xs_ref[...], g_ref[...]).astype(o_ref.dtype)


def _norm_cast(xp, xs, gain):
    d = xp.shape[1]
    m = xp.shape[0] + xs.shape[0]
    tm, np_tiles, prompt, sample, stacked, vec = _split_rows(xp.shape[0], xs.shape[0], d, 256)
    return pl.pallas_call(
        functools.partial(_norm_cast_kernel, np_tiles=np_tiles),
        grid=(m // tm,),
        in_specs=[prompt, sample, vec],
        out_specs=stacked,
        out_shape=jax.ShapeDtypeStruct((m, d), BF16),
        compiler_params=_params("arbitrary"),
        name="norm_cast",
    )(xp, xs, gain.reshape(1, d))


def _resid_norm_kernel(xp_ref, xs_ref, y_ref, gpost_ref, gpre_ref, x1_ref, h_ref, *, np_tiles):
    i = pl.program_id(0)

    def body(x_ref):
        x1 = x_ref[...] + _rms(y_ref[...], gpost_ref[...])
        x1_ref[...] = x1
        h_ref[...] = _rms(x1, gpre_ref[...]).astype(h_ref.dtype)

    pl.when(i < np_tiles)(lambda: body(xp_ref))
    pl.when(i >= np_tiles)(lambda: body(xs_ref))


def _resid_norm(xp, xs, y, gain_post, gain_pre):
    m, d = y.shape
    tm, np_tiles, prompt, sample, stacked, vec = _split_rows(xp.shape[0], xs.shape[0], d, 256)
    return pl.pallas_call(
        functools.partial(_resid_norm_kernel, np_tiles=np_tiles),
        grid=(m // tm,),
        in_specs=[prompt, sample, stacked, vec, vec],
        out_specs=[stacked, stacked],
        out_shape=[jax.ShapeDtypeStruct((m, d), F32), jax.ShapeDtypeStruct((m, d), BF16)],
        compiler_params=_params("arbitrary"),
        name="resid_norm",
    )(xp, xs, y, gain_post.reshape(1, d), gain_pre.reshape(1, d))


def _resid_final_kernel(x_ref, y_ref, g_ref, op_ref, os_ref, *, np_tiles):
    i = pl.program_id(0)

    @pl.when(i < np_tiles)
    def _():
        op_ref[...] = x_ref[...] + _rms(y_ref[...], g_ref[...])

    @pl.when(i >= np_tiles)
    def _():
        os_ref[...] = x_ref[...] + _rms(y_ref[...], g_ref[...])


def _resid_final(x, y, gain, n_prompt):
    m, d = x.shape
    tm, np_tiles, prompt, sample, stacked, vec = _split_rows(n_prompt, m - n_prompt, d, 256)
    return pl.pallas_call(
        functools.partial(_resid_final_kernel, np_tiles=np_tiles),
        grid=(m // tm,),
        in_specs=[stacked, stacked, vec],
        out_specs=[prompt, sample],
        out_shape=[jax.ShapeDtypeStruct((n_prompt, d), F32),
                   jax.ShapeDtypeStruct((m - n_prompt, d), F32)],
        compiler_params=_params("arbitrary"),
        name="resid_final",
    )(x, y, gain.reshape(1, d))


def _buffer_update_step(step, src_ref, new_ref, dst_ref, sem):
    nseq, rows, _ = src_ref.shape
    shift = new_ref.shape[1]

    def copies(b):
        slot = b % 2
        body = pltpu.make_async_copy(src_ref.at[b, pl.ds(shift, rows - shift)],
                                     dst_ref.at[b, pl.ds(0, rows - shift)], sem.at[slot, 0])
        tail = pltpu.make_async_copy(new_ref.at[b], dst_ref.at[b, pl.ds(rows - shift, shift)],
                                     sem.at[slot, 1])
        return body, tail

    @pl.when((step >= 2) & (step < nseq + 2))
    def _():
        for c in copies(step - 2):
            c.wait()

    @pl.when(step < nseq)
    def _():
        for c in copies(step):
            c.start(priority=1)


def _mm_kernel(*refs, act, update):
    if update:
        a_ref, w_ref, src_ref, new_ref, o_ref, dst_ref, sem = refs
        step = pl.program_id(0) * pl.num_programs(1) + pl.program_id(1)
        _buffer_update_step(step, src_ref, new_ref, dst_ref, sem)
    else:
        a_ref, w_ref, o_ref = refs
    acc = _dot(a_ref[...], w_ref[...])
    if act == "relu2":
        acc = jnp.maximum(acc, 0.0)
        acc = acc * acc
    o_ref[...] = acc.astype(o_ref.dtype)


def _mm_bias_sigmoid_kernel(a_ref, w_ref, b_ref, o_ref):
    o_ref[...] = _sigmoid(_dot(a_ref[...], w_ref[...]) + b_ref[...]).astype(o_ref.dtype)


def _update_args(update, nsteps):
    if update is None:
        return (), [], [], [], []
    src, new = update
    assert nsteps >= src.shape[0] + 2, "grid too short to finish the buffer update"
    any_spec = pl.BlockSpec(memory_space=pl.ANY)
    return ((src, new), [any_spec, any_spec], [any_spec],
            [jax.ShapeDtypeStruct(src.shape, src.dtype)], [pltpu.SemaphoreType.DMA((2, 2))])


def _matmul(a, w, *, out_dtype=F32, act=None, bias=None, update=None, name="matmul"):
    m, k = a.shape
    n = w.shape[1]
    tm = _tile(m, 1024, SUBLANES)
    tn = _tile(n, 1024, LANES)
    grid = (m // tm, n // tn)
    a_spec = pl.BlockSpec((tm, k), lambda i, j: (i, 0))
    w_spec = pl.BlockSpec((k, tn), lambda i, j: (0, j))
    o_spec = pl.BlockSpec((tm, tn), lambda i, j: (i, j))
    u_in, u_specs, u_out_specs, u_shapes, scratch = _update_args(update, grid[0] * grid[1])
    if bias is None:
        body = functools.partial(_mm_kernel, act=act, update=update is not None)
        ins, specs = (a, w, *u_in), [a_spec, w_spec, *u_specs]
    else:
        assert update is None
        body, ins = _mm_bias_sigmoid_kernel, (a, w, bias.reshape(1, n))
        specs = [a_spec, w_spec, pl.BlockSpec((1, tn), lambda i, j: (0, j))]
    out = pl.pallas_call(
        body,
        grid=grid,
        in_specs=specs,
        out_specs=[o_spec, *u_out_specs],
        out_shape=[jax.ShapeDtypeStruct((m, n), out_dtype), *u_shapes],
        scratch_shapes=scratch,
        compiler_params=_params("arbitrary", "arbitrary"),
        name=name,
    )(*ins)
    return out if update is not None else out[0]


def _mm_acc_kernel(*refs, update):
    if update:
        a_ref, w_ref, src_ref, new_ref, o_ref, dst_ref, sem = refs
        step = ((pl.program_id(0) * pl.num_programs(1) + pl.program_id(1)) * pl.num_programs(2)
                + pl.program_id(2))
        _buffer_update_step(step, src_ref, new_ref, dst_ref, sem)
    else:
        a_ref, w_ref, o_ref = refs

    @pl.when(pl.program_id(2) == 0)
    def _():
        o_ref[...] = jnp.zeros_like(o_ref)

    o_ref[...] += _dot(a_ref[...], w_ref[...])


def _matmul_ksplit(a, w, *, tk, update=None, name):
    m, k = a.shape
    n = w.shape[1]
    tm = _tile(m, 1024, SUBLANES)
    tn = _tile(n, 1024, LANES)
    tk = _tile(k, tk, LANES)
    grid = (m // tm, n // tn, k // tk)
    u_in, u_specs, u_out_specs, u_shapes, scratch = _update_args(update, grid[0] * grid[1] * grid[2])
    out = pl.pallas_call(
        functools.partial(_mm_acc_kernel, update=update is not None),
        grid=grid,
        in_specs=[pl.BlockSpec((tm, tk), lambda i, j, l: (i, l)),
                  pl.BlockSpec((tk, tn), lambda i, j, l: (l, j)), *u_specs],
        out_specs=[pl.BlockSpec((tm, tn), lambda i, j, l: (i, j)), *u_out_specs],
        out_shape=[jax.ShapeDtypeStruct((m, n), F32), *u_shapes],
        scratch_shapes=scratch,
        compiler_params=_params("arbitrary", "arbitrary", "arbitrary"),
        name=name,
    )(a, w, *u_in)
    return out if update is not None else out[0]


def _merge_kernel(att_ref, hg_ref, wa_ref, wh_ref, ga_ref, gh_ref, o_ref):
    pa = _dot(att_ref[...], wa_ref[...])
    ph = _dot(hg_ref[...], wh_ref[...])
    o_ref[...] = (ga_ref[...] * pa + gh_ref[...] * ph).astype(o_ref.dtype)


def _merge_proj(att, hg, w_att, w_hg, gates):
    m, ka = att.shape
    kh = hg.shape[1]
    d = w_att.shape[1]
    tm = _tile(m, 1024, SUBLANES)
    tn = _tile(d, 1024, LANES)
    nj = d // tn
    return pl.pallas_call(
        _merge_kernel,
        grid=(m // tm, nj),
        in_specs=[pl.BlockSpec((tm, ka), lambda i, j: (i, 0)),
                  pl.BlockSpec((tm, kh), lambda i, j: (i, 0)),
                  pl.BlockSpec((ka, tn), lambda i, j: (0, j)),
                  pl.BlockSpec((kh, tn), lambda i, j: (0, j)),
                  pl.BlockSpec((tm, tn), lambda i, j: (i, j)),
                  pl.BlockSpec((tm, tn), lambda i, j: (i, j + nj))],
        out_specs=pl.BlockSpec((tm, tn), lambda i, j: (i, j)),
        out_shape=jax.ShapeDtypeStruct((m, d), BF16),
        compiler_params=_params("parallel", "arbitrary"),
        name="merge_proj",
    )(att, hg, w_att, w_hg, gates, gates)


def _attn_prompt_kernel(q_ref, kp_ref, kc_ref, vp_ref, vc_ref, o_ref, lse_ref, *, dilation):
    jb = pl.program_id(1)
    blk = ATT_BLOCK
    r = lax.broadcasted_iota(jnp.int32, (blk, 2 * blk), 0)
    c = lax.broadcasted_iota(jnp.int32, (blk, 2 * blk), 1)
    dist = blk + r - c
    mask = (dist >= 0) & (dist <= blk) & ((c >= blk) | (jb > 0))
    for res in range(dilation):
        rows = pl.ds(res, blk, stride=dilation) if dilation > 1 else slice(None)
        for h in range(q_ref.shape[1] // ATT_HEAD_DIM):
            sl = slice(h * ATT_HEAD_DIM, (h + 1) * ATT_HEAD_DIM)
            q = q_ref[rows, sl].astype(BF16)
            k = jnp.concatenate([kp_ref[rows, sl], kc_ref[rows, sl]], axis=0).astype(BF16)
            v = jnp.concatenate([vp_ref[rows, sl], vc_ref[rows, sl]], axis=0).astype(BF16)
            s = jnp.where(mask, _dot_nt(q, k) * ATT_SCALE, NEG_BIG)
            m = jnp.max(s, axis=-1, keepdims=True)
            p = jnp.exp(s - m)
            l = jnp.sum(p, axis=-1, keepdims=True)
            o_ref[rows, sl] = _dot(p.astype(BF16), v) / l
            lse_ref[rows, sl] = jnp.broadcast_to(m + jnp.log(l), (blk, ATT_HEAD_DIM))


def _attn_prompt(proj, seq, group, dilation):
    chunk = ATT_BLOCK * dilation
    nch = seq // chunk
    lw = ATT_OUT_WIDTH if dilation == 1 else ATT_HEAD_DIM
    nhb = ATT_OUT_WIDTH // lw
    ngroups = len(ATT_GROUPS)

    def spec(col, prev):
        c0 = col * nhb
        if prev:
            return pl.BlockSpec((chunk, lw), lambda hb, j: (jnp.maximum(j - 1, 0), c0 + hb))
        return pl.BlockSpec((chunk, lw), lambda hb, j: (j, c0 + hb))

    out_spec = pl.BlockSpec((chunk, lw), lambda hb, j: (j, hb))
    shape = jax.ShapeDtypeStruct((seq, ATT_OUT_WIDTH), F32)
    return pl.pallas_call(
        functools.partial(_attn_prompt_kernel, dilation=dilation),
        grid=(nhb, nch),
        in_specs=[spec(group, False),
                  spec(ngroups + group, True), spec(ngroups + group, False),
                  spec(2 * ngroups + group, True), spec(2 * ngroups + group, False)],
        out_specs=[out_spec, out_spec],
        out_shape=[shape, shape],
        compiler_params=_params("parallel", "arbitrary"),
        name=f"attn_prompt_d{dilation}",
    )(proj, proj, proj, proj, proj)


def _attn_sample_kernel(*refs, dilation, t_new, update):
    if update:
        (q_ref, kn_ref, vn_ref, kb_ref, vb_ref, kn2_ref, vn2_ref, kx_ref, vx_ref,
         o_ref, lse_ref, ko_ref, vo_ref, m_sc, l_sc, acc_sc) = refs
    else:
        q_ref, kn_ref, vn_ref, kb_ref, vb_ref, o_ref, lse_ref, m_sc, l_sc, acc_sc = refs
    lt = pl.program_id(1)
    nlt = pl.num_programs(1)
    nh = ATT_SLOTS
    tl = kb_ref.shape[1] // nh
    shift = t_new * nh
    dmask = dilation - 1
    rows = nh * t_new

    def lanes(h):
        return slice(h * ATT_HEAD_DIM, (h + 1) * ATT_HEAD_DIM)

    def per_head(fn):
        return jnp.concatenate([fn(h) for h in range(nh)], axis=0)

    def head_rows(x, h):
        return x[h * t_new:(h + 1) * t_new]

    q16 = [q_ref[0, :, lanes(h)].astype(BF16) for h in range(nh)]

    @pl.when(lt == 0)
    def _():
        s = per_head(lambda h: _dot_nt(q16[h], kn_ref[0, :, lanes(h)].astype(BF16))) * ATT_SCALE
        t = lax.broadcasted_iota(jnp.int32, (rows, t_new), 0) % t_new
        i = lax.broadcasted_iota(jnp.int32, (rows, t_new), 1)
        s = jnp.where((i <= t) & (((t - i) & dmask) == 0), s, NEG_BIG)
        m = jnp.max(s, axis=-1, keepdims=True)
        p = jnp.exp(s - m)
        m_sc[...] = m
        l_sc[...] = jnp.sum(p, axis=-1, keepdims=True)
        acc_sc[...] = per_head(lambda h: _dot(head_rows(p, h).astype(BF16),
                                              vn_ref[0, :, lanes(h)].astype(BF16)))

    s = per_head(lambda h: _dot_nt(q16[h], kb_ref[0, pl.ds(h, tl, stride=nh), :].astype(BF16))) * ATT_SCALE
    t = lax.broadcasted_iota(jnp.int32, (rows, tl), 0) % t_new
    j = lax.broadcasted_iota(jnp.int32, (rows, tl), 1) + lt * tl
    s = jnp.where((j >= t) & (((j - t) & dmask) == 0), s, NEG_BIG)
    m_old = m_sc[...]
    m_new = jnp.maximum(m_old, jnp.max(s, axis=-1, keepdims=True))
    alpha = jnp.exp(m_old - m_new)
    p = jnp.exp(s - m_new)
    l_sc[...] = alpha * l_sc[...] + jnp.sum(p, axis=-1, keepdims=True)
    pv = per_head(lambda h: _dot(head_rows(p, h).astype(BF16),
                                 vb_ref[0, pl.ds(h, tl, stride=nh), :].astype(BF16)))
    acc_sc[...] = alpha * acc_sc[...] + pv
    m_sc[...] = m_new

    if update:
        keep = tl * nh - shift
        ko_ref[0, :keep] = kb_ref[0, shift:]
        vo_ref[0, :keep] = vb_ref[0, shift:]

        @pl.when(lt < nlt - 1)
        def _():
            ko_ref[0, keep:] = kx_ref[0]
            vo_ref[0, keep:] = vx_ref[0]

        @pl.when(lt == nlt - 1)
        def _():
            ko_ref[0, keep:] = kn2_ref[0]
            vo_ref[0, keep:] = vn2_ref[0]

    @pl.when(lt == nlt - 1)
    def _():
        l = l_sc[...]
        out = acc_sc[...] / l
        lse = m_sc[...] + jnp.log(l)
        for h in range(nh):
            o_ref[0, :, lanes(h)] = head_rows(out, h)
            lse_ref[0, :, lanes(h)] = jnp.broadcast_to(head_rows(lse, h), (t_new, ATT_HEAD_DIM))


def _new_rows(proj3, batch0, col):
    db = proj3.shape[0] - batch0
    c0 = col * ATT_OUT_WIDTH
    return proj3[batch0:, :, c0:c0 + ATT_OUT_WIDTH].reshape(db, proj3.shape[1] * ATT_SLOTS, ATT_HEAD_DIM)


def _attn_sample(proj3, batch0, group, dilation, k_buf, v_buf, *, update):
    db, rows_total, _ = k_buf.shape
    nh = ATT_SLOTS
    length = rows_total // nh
    t_new = proj3.shape[1]
    ngroups = len(ATT_GROUPS)
    tl = _tile(length, 1024 if update else 2048, SUBLANES)
    nlt = length // tl
    step = tl // t_new
    last = length // t_new - 1
    shift = t_new * nh

    def new_spec(col):
        return pl.BlockSpec((1, t_new, ATT_OUT_WIDTH), lambda b, l: (batch0 + b, 0, col))

    new2_spec = pl.BlockSpec((1, shift, ATT_HEAD_DIM), lambda b, l: (b, 0, 0))
    buf_spec = pl.BlockSpec((1, tl * nh, ATT_HEAD_DIM), lambda b, l: (b, l, 0))
    next_spec = pl.BlockSpec((1, shift, ATT_HEAD_DIM),
                             lambda b, l: (b, jnp.minimum((l + 1) * step, last), 0))
    small = pl.BlockSpec((1, t_new, ATT_OUT_WIDTH), lambda b, l: (b, 0, 0))
    small_shape = jax.ShapeDtypeStruct((db, t_new, ATT_OUT_WIDTH), F32)
    buf_shape = jax.ShapeDtypeStruct(k_buf.shape, F32)
    rows = nh * t_new
    ins = [proj3, proj3, proj3, k_buf, v_buf]
    in_specs = [new_spec(group), new_spec(ngroups + group), new_spec(2 * ngroups + group),
                buf_spec, buf_spec]
    out_specs, out_shape = [small, small], [small_shape, small_shape]
    if update:
        ins += [_new_rows(proj3, batch0, ngroups + group), _new_rows(proj3, batch0, 2 * ngroups + group),
                k_buf, v_buf]
        in_specs += [new2_spec, new2_spec, next_spec, next_spec]
        out_specs += [buf_spec, buf_spec]
        out_shape += [buf_shape, buf_shape]
    return pl.pallas_call(
        functools.partial(_attn_sample_kernel, dilation=dilation, t_new=t_new, update=update),
        grid=(db, nlt),
        in_specs=in_specs,
        out_specs=out_specs,
        out_shape=out_shape,
        scratch_shapes=[pltpu.VMEM((rows, 1), F32), pltpu.VMEM((rows, 1), F32),
                        pltpu.VMEM((rows, ATT_HEAD_DIM), F32)],
        compiler_params=_params("parallel", "arbitrary"),
        name=f"attn_sample_d{dilation}",
    )(*ins)


def _combine_kernel(o0, o1, o2, l0, l1, l2, out_ref):
    a, b, c = l0[...], l1[...], l2[...]
    m = jnp.maximum(jnp.maximum(a, b), c)
    ea, eb, ec = jnp.exp(a - m), jnp.exp(b - m), jnp.exp(c - m)
    num = ea * o0[...] + eb * o1[...] + ec * o2[...]
    out_ref[...] = (num / (ea + eb + ec)).astype(out_ref.dtype)


def _combine(outs, lses):
    m, w = outs[0].shape
    tm = _tile(m, 512, SUBLANES)
    spec = pl.BlockSpec((tm, w), lambda i: (i, 0))
    return pl.pallas_call(
        _combine_kernel,
        grid=(m // tm,),
        in_specs=[spec] * 6,
        out_specs=spec,
        out_shape=jax.ShapeDtypeStruct((m, w), BF16),
        compiler_params=_params("parallel"),
        name="attn_combine",
    )(*outs, *lses)


def _cumsum_rows(x):
    c = x.shape[0]
    r = lax.broadcasted_iota(jnp.int32, (c, c), 0)
    s = lax.broadcasted_iota(jnp.int32, (c, c), 1)
    tri = jnp.where(r >= s, 1.0, 0.0).astype(BF16)
    hi = x.astype(BF16)
    rem = x - hi.astype(F32)
    mid = rem.astype(BF16)
    lo = (rem - mid.astype(F32)).astype(BF16)
    return _dot(tri, hi) + _dot(tri, mid) + _dot(tri, lo)


def _col_bcast(w):
    hi = w.astype(BF16).astype(F32)
    mid = (w - hi).astype(BF16).astype(F32)
    lo = (w - hi - mid).astype(BF16).astype(F32)
    terms = jnp.concatenate([hi, mid, lo, jnp.zeros((SUBLANES - 3, LANES), F32)], axis=0).astype(BF16)
    ones = jnp.where(lax.broadcasted_iota(jnp.int32, (SUBLANES, LANES), 0) < 3, 1.0, 0.0).astype(BF16)
    return _dot_tn(terms, ones)


def _lower_bound(lb_ref):
    a = lb_ref[...]
    e = jnp.exp(a - jnp.max(a, axis=0, keepdims=True))
    return e[0:1] / jnp.sum(e, axis=0, keepdims=True)


def _hgrn_chunk(qg, fg, ig, gg, lb, gain, st):
    c = qg.shape[0]
    q = qg * _sigmoid(qg) * HG_SCALE
    forget = lb + (1.0 - lb) * _sigmoid(fg)
    k = 1.0 - forget
    v = ig
    b = _cumsum_rows(jnp.log(forget))
    v16 = v.astype(BF16)

    nsub = c // HG_SUB
    b3 = b.reshape(nsub, HG_SUB, LANES)
    q3 = q.reshape(nsub, HG_SUB, LANES)
    k3 = k.reshape(nsub, HG_SUB, LANES)
    pos = lax.broadcasted_iota(jnp.int32, (nsub, HG_SUB, LANES), 1)
    row = lax.broadcasted_iota(jnp.int32, (c, c), 0)
    col = lax.broadcasted_iota(jnp.int32, (c, c), 1)
    scores = jnp.zeros((c, c), F32)
    for s in range(HG_SUB):
        diff = jnp.where(pos >= s, b3 - b3[:, s:s + 1, :], NEG_BIG)
        w = jnp.sum(q3 * k3[:, s:s + 1, :] * jnp.exp(diff), axis=-1, keepdims=True)
        w = jnp.broadcast_to(w.reshape(c, 1), (c, c))
        scores = jnp.where(col == (row // HG_SUB) * HG_SUB + s, w, scores)

    width = HG_SUB
    while width < c:
        pair = 2 * width
        bm = jnp.concatenate(
            [jnp.broadcast_to(b[p * pair + width - 1:p * pair + width], (pair, LANES))
             for p in range(c // pair)], axis=0)
        e = jnp.exp(-jnp.abs(b - bm))
        right = (lax.broadcasted_iota(jnp.int32, (c, LANES), 0) // width) % 2 == 1
        ql = jnp.where(right, q * e, 0.0).astype(BF16)
        kl = jnp.where(right, 0.0, k * e).astype(BF16)
        a = _dot_nt(ql, kl)
        scores = jnp.where((row // pair == col // pair) & (row // width != col // width) & (row > col),
                           a, scores)
        width = pair

    o = _dot(scores.astype(BF16), v16) + _dot_nt((q * jnp.exp(b)).astype(BF16), st.astype(BF16))
    b_last = b[c - 1:c]
    st_new = st * jnp.exp(b_last) + _dot_tn(v16, (k * jnp.exp(b_last - b)).astype(BF16))
    out = _rms(o, gain) * (gg * _sigmoid(gg))
    return out, st_new


def _hgrn_prompt_kernel(q_ref, f_ref, i_ref, g_ref, lb_ref, gain_ref, o_ref, s_ref, st_sc):
    tb = pl.program_id(1)

    @pl.when(tb == 0)
    def _():
        st_sc[...] = jnp.zeros_like(st_sc)

    nheads = st_sc.shape[0]
    gain = gain_ref[...]
    lbs = [_lower_bound(lb_ref.at[:, h * LANES:(h + 1) * LANES]) for h in range(nheads)]
    sts = [st_sc[h] for h in range(nheads)]
    for ci in range(q_ref.shape[0] // HG_CHUNK):
        rs = slice(ci * HG_CHUNK, (ci + 1) * HG_CHUNK)
        for h in range(nheads):
            sl = slice(h * LANES, (h + 1) * LANES)
            out, sts[h] = _hgrn_chunk(q_ref[rs, sl], f_ref[rs, sl], i_ref[rs, sl], g_ref[rs, sl],
                                      lbs[h], gain, sts[h])
            o_ref[rs, sl] = out.astype(o_ref.dtype)
    for h in range(nheads):
        st_sc[h] = sts[h]

    @pl.when(tb == pl.num_programs(1) - 1)
    def _():
        for h in range(nheads):
            s_ref[h] = sts[h].T


def _hgrn_prompt(proj, seq, hg_lower_bound, hg_norm, hg_width):
    heads = hg_width // HG_EXPAND
    hw = HG_EXPAND
    assert 3 * ATT_WIDTH % hw == 0
    base = 3 * ATT_WIDTH // hw
    per = hg_width // hw
    hpb = hw // HG_EXPAND
    tb = _tile(seq, 512, HG_CHUNK)

    def col(which):
        return pl.BlockSpec((tb, hw), lambda g, t: (t, base + which * per + g))

    depth1 = hg_lower_bound.shape[0]
    return pl.pallas_call(
        _hgrn_prompt_kernel,
        grid=(per, seq // tb),
        in_specs=[col(0), col(1), col(2), col(3),
                  pl.BlockSpec((depth1, hw), lambda g, t: (0, g)),
                  pl.BlockSpec((1, LANES), lambda g, t: (0, 0))],
        out_specs=[pl.BlockSpec((tb, hw), lambda g, t: (t, g)),
                   pl.BlockSpec((hpb, HG_EXPAND, HG_HEAD_V), lambda g, t: (g, 0, 0))],
        out_shape=[jax.ShapeDtypeStruct((seq, hg_width), BF16),
                   jax.ShapeDtypeStruct((heads, HG_EXPAND, HG_HEAD_V), F32)],
        scratch_shapes=[pltpu.VMEM((hpb, HG_EXPAND, HG_HEAD_V), F32)],
        compiler_params=_params("parallel", "arbitrary"),
        name="hgrn_prompt",
    )(proj, proj, proj, proj, hg_lower_bound, hg_norm.reshape(1, LANES))


def _hgrn_sample_kernel(q_ref, f_ref, i_ref, g_ref, lb_ref, gain_ref, s_ref, o_ref, so_ref):
    nheads = s_ref.shape[1]
    c = q_ref.shape[1]
    qg, fg, v, gg = q_ref[0], f_ref[0], i_ref[0], g_ref[0]
    lb = _lower_bound(lb_ref)
    gain = gain_ref[...]
    q = qg * _sigmoid(qg) * HG_SCALE
    forget = lb + (1.0 - lb) * _sigmoid(fg)
    k = 1.0 - forget
    b = _cumsum_rows(jnp.log(forget))
    b_last = b[c - 1:c]
    qdec = (q * jnp.exp(b)).astype(BF16)
    kdec = (k * jnp.exp(b_last - b)).astype(BF16)
    decay = jnp.exp(b_last)
    v16 = v.astype(BF16)
    t = lax.broadcasted_iota(jnp.int32, b.shape, 0)
    pair = [q * k[s:s + 1] * jnp.exp(jnp.where(t >= s, b - b[s:s + 1], NEG_BIG)) for s in range(c)]
    outs = []
    for h in range(nheads):
        sl = slice(h * LANES, (h + 1) * LANES)
        st = s_ref[0, h]
        o = _dot(qdec[:, sl], st.astype(BF16))
        for s in range(c):
            o = o + jnp.sum(pair[s][:, sl], axis=-1, keepdims=True) * v[s:s + 1, sl]
        so_ref[0, h] = st * _col_bcast(decay[:, sl]) + _dot_tn(kdec[:, sl], v16[:, sl])
        outs.append(_rms(o, gain))
    o_ref[0] = (jnp.concatenate(outs, axis=1) * (gg * _sigmoid(gg))).astype(o_ref.dtype)


def _hgrn_sample(proj3, batch0, state, hg_lower_bound, hg_norm):
    db, heads = state.shape[:2]
    t_new = proj3.shape[1]
    hg_width = heads * HG_EXPAND
    cw = _tile(hg_width, ATT_OUT_WIDTH, LANES)
    assert 3 * ATT_WIDTH % cw == 0
    base = 3 * ATT_WIDTH // cw
    per = hg_width // cw
    hb = cw // HG_EXPAND

    def col(which):
        return pl.BlockSpec((1, t_new, cw), lambda b, c: (batch0 + b, 0, base + which * per + c))

    depth1 = hg_lower_bound.shape[0]
    st_spec = pl.BlockSpec((1, hb, HG_EXPAND, HG_HEAD_V), lambda b, c: (b, c, 0, 0))
    return pl.pallas_call(
        _hgrn_sample_kernel,
        grid=(db, per),
        in_specs=[col(0), col(1), col(2), col(3),
                  pl.BlockSpec((depth1, cw), lambda b, c: (0, c)),
                  pl.BlockSpec((1, LANES), lambda b, c: (0, 0)),
                  st_spec],
        out_specs=[pl.BlockSpec((1, t_new, cw), lambda b, c: (b, 0, c)), st_spec],
        out_shape=[jax.ShapeDtypeStruct((db, t_new, hg_width), BF16),
                   jax.ShapeDtypeStruct(state.shape, F32)],
        compiler_params=_params("parallel", "parallel"),
        name="hgrn_sample",
    )(proj3, proj3, proj3, proj3, hg_lower_bound, hg_norm.reshape(1, LANES), state)


def kernel(x_prompt, x_sample, cache_k_w128, cache_v_w128, cache_k_w512, cache_v_w512,
           cache_k_w2048, cache_v_w2048, state_hgrn, hg_lower_bound, w_in, w_gate, b_gate,
           w_proj_att, w_proj_hg, w_out, hg_norm, norm_mix_pre, norm_mix_post,
           norm_ffn_pre, norm_ffn_post, w_up, w_down):
    assert w_in.shape[0] == 1, "single-layer trunk"
    batch, seq, d_model = x_prompt.shape
    db, t_new, _ = x_sample.shape
    assert batch == 1
    hg_width = w_proj_hg.shape[1]
    n_prompt = batch * seq
    n_sample = db * t_new
    caches = (cache_k_w128, cache_v_w128, cache_k_w512, cache_v_w512, cache_k_w2048, cache_v_w2048)

    xp = x_prompt.reshape(n_prompt, d_model)
    xs = x_sample.reshape(n_sample, d_model)
    h = _norm_cast(xp, xs, norm_mix_pre[0])
    proj = _matmul(h, w_in[0].astype(BF16), name="in_proj")
    gates = _matmul(h, w_gate[0].astype(BF16), bias=b_gate[0], name="gate_proj")
    in_width = proj.shape[1]
    proj3 = proj.reshape((n_prompt + n_sample) // t_new, t_new, in_width)
    batch0 = n_prompt // t_new

    ngroups = len(ATT_GROUPS)
    mlp_steps = ((n_prompt + n_sample) // _tile(n_prompt + n_sample, 1024, SUBLANES)
                 * (w_up.shape[2] // _tile(w_up.shape[2], 1024, LANES)))
    pending = [None, None]
    outs_p, lses_p, outs_s, lses_s, new_kv_p, new_kv_s = [], [], [], [], [], []
    for g, (window, dilation) in enumerate(ATT_GROUPS):
        o, lse = _attn_prompt(proj, seq, g, dilation)
        outs_p.append(o)
        lses_p.append(lse)
        keep = min(window, seq)
        for part in (1, 2):
            c0 = part * ATT_WIDTH + g * ATT_OUT_WIDTH
            rows = proj[n_prompt - keep:n_prompt, c0:c0 + ATT_OUT_WIDTH]
            new_kv_p.append(rows.reshape(1, batch, keep, ATT_SLOTS, ATT_HEAD_DIM))
        k_buf, v_buf = caches[2 * g][0], caches[2 * g + 1][0]
        length = k_buf.shape[1]
        assert length == window and length == dilation * ATT_BLOCK
        k_buf = k_buf.reshape(db, length * ATT_SLOTS, ATT_HEAD_DIM)
        v_buf = v_buf.reshape(db, length * ATT_SLOTS, ATT_HEAD_DIM)
        deferred = db + 2 <= mlp_steps and g == ngroups - 1
        res = _attn_sample(proj3, batch0, g, dilation, k_buf, v_buf, update=not deferred)
        outs_s.append(res[0].reshape(n_sample, ATT_OUT_WIDTH))
        lses_s.append(res[1].reshape(n_sample, ATT_OUT_WIDTH))
        if deferred:
            pending = [(k_buf, _new_rows(proj3, batch0, ngroups + g)),
                       (v_buf, _new_rows(proj3, batch0, 2 * ngroups + g))]
        else:
            new_kv_s += list(res[2:])
    att = jnp.concatenate([_combine(outs_p, lses_p), _combine(outs_s, lses_s)], axis=0)

    hg_p, state_p = _hgrn_prompt(proj, seq, hg_lower_bound, hg_norm[0], hg_width)
    hg_s, state_s = _hgrn_sample(proj3, batch0, state_hgrn[0], hg_lower_bound, hg_norm[0])
    hg = jnp.concatenate([hg_p, hg_s.reshape(n_sample, hg_width)], axis=0)

    merged = _merge_proj(att, hg, w_proj_att[0].astype(BF16), w_proj_hg[0].astype(BF16), gates)
    mixed = _matmul(merged, w_out[0].astype(BF16), name="out_proj")
    x1, h2 = _resid_norm(xp, xs, mixed, norm_mix_post[0], norm_ffn_pre[0])

    u = _matmul(h2, w_up[0].astype(BF16), out_dtype=BF16, act="relu2", update=pending[0], name="ffn_up")
    if pending[0] is not None:
        u, k_last = u
    z = _matmul_ksplit(u, w_down[0].astype(BF16), tk=4096, update=pending[1], name="ffn_down")
    if pending[1] is not None:
        z, v_last = z
        new_kv_s += [k_last, v_last]
    y_prompt, y_sample = _resid_final(x1, z, norm_ffn_post[0], n_prompt)
    y_prompt = y_prompt.reshape(batch, seq, d_model)
    y_sample = y_sample.reshape(db, t_new, d_model)
    heads = hg_width // HG_EXPAND
    new_kv_s = [c.reshape(src.shape) for c, src in zip(new_kv_s, caches)]
    return (y_prompt, y_sample, *new_kv_p,
            state_p.reshape(1, batch, heads, HG_EXPAND, HG_HEAD_V),
            *new_kv_s, state_s[None])
```

```python
import functools
import math

import jax
import jax.numpy as jnp
from jax import lax
from jax.experimental import pallas as pl
from jax.experimental.pallas import tpu as pltpu

ATT_HEAD_DIM = 128
ATT_SLOTS = 8
ATT_GROUPS = ((128, 1), (512, 4), (2048, 16))
ATT_BLOCK = 128
ATT_OUT_WIDTH = ATT_SLOTS * ATT_HEAD_DIM
ATT_WIDTH = len(ATT_GROUPS) * ATT_OUT_WIDTH
ATT_SCALE = ATT_HEAD_DIM ** -0.5
HG_EXPAND = 128
HG_HEAD_V = 128
HG_SCALE = HG_EXPAND ** -0.5
HG_CHUNK = 64
HG_SUB = 8
RMS_EPS = 1e-6
NEG_BIG = -1e30

V7X_VMEM_BYTES = 64 * 1024 * 1024
VMEM_LIMIT = V7X_VMEM_BYTES - 8 * 1024 * 1024
LANES = 128
SUBLANES = 8

BF16 = jnp.bfloat16
F32 = jnp.float32


def _params(*sem):
    return pltpu.CompilerParams(dimension_semantics=sem, vmem_limit_bytes=VMEM_LIMIT)


def _tile(n, target, mult):
    if n <= target:
        return n
    t = (target // mult) * mult
    while t >= mult:
        if n % t == 0:
            return t
        t -= mult
    raise ValueError(f"no tile for {n}")


def _sigmoid(x):
    return 1.0 / (1.0 + jnp.exp(-x))


def _dot(a, b):
    return jnp.dot(a, b, preferred_element_type=F32)


def _dot_nt(a, b):
    return lax.dot_general(a, b, (((1,), (1,)), ((), ())), preferred_element_type=F32)


def _dot_tn(a, b):
    return lax.dot_general(a, b, (((0,), (0,)), ((), ())), preferred_element_type=F32)


def _rms(x, gain):
    return x * lax.rsqrt(jnp.mean(x * x, axis=-1, keepdims=True) + RMS_EPS) * gain


def _split_rows(n_prompt, n_sample, d, target):
    tm = _tile(math.gcd(n_prompt, n_sample), target, SUBLANES)
    np_tiles = n_prompt // tm
    prompt = pl.BlockSpec((tm, d), lambda i: (jnp.minimum(i, np_tiles - 1), 0))
    sample = pl.BlockSpec((tm, d), lambda i: (jnp.maximum(i - np_tiles, 0), 0))
    stacked = pl.BlockSpec((tm, d), lambda i: (i, 0))
    vec = pl.BlockSpec((1, d), lambda i: (0, 0))
    return tm, np_tiles, prompt, sample, stacked, vec


def _norm_cast_kernel(xp_ref, xs_ref, g_ref, o_ref, *, np_tiles):
    i = pl.program_id(0)

    @pl.when(i < np_tiles)
    def _():
        o_ref[...] = _rms(xp_ref[...], g_ref[...]).astype(o_ref.dtype)

    @pl.when(i >= np_tiles)
    def _():
        o_ref[...] = _rms(xs_ref[...], g_ref[...]).astype(o_ref.dtype)


def _norm_cast(xp, xs, gain):
    d = xp.shape[1]
    m = xp.shape[0] + xs.shape[0]
    tm, np_tiles, prompt, sample, stacked, vec = _split_rows(xp.shape[0], xs.shape[0], d, 256)
    return pl.pallas_call(
        functools.partial(_norm_cast_kernel, np_tiles=np_tiles),
        grid=(m // tm,),
        in_specs=[prompt, sample, vec],
        out_specs=stacked,
        out_shape=jax.ShapeDtypeStruct((m, d), BF16),
        compiler_params=_params("arbitrary"),
        name="norm_cast",
    )(xp, xs, gain.reshape(1, d))


def _resid_norm_kernel(xp_ref, xs_ref, y_ref, gpost_ref, gpre_ref, x1_ref, h_ref, *, np_tiles):
    i = pl.program_id(0)

    def body(x_ref):
        x1 = x_ref[...] + _rms(y_ref[...], gpost_ref[...])
        x1_ref[...] = x1
        h_ref[...] = _rms(x1, gpre_ref[...]).astype(h_ref.dtype)

    pl.when(i < np_tiles)(lambda: body(xp_ref))
    pl.when(i >= np_tiles)(lambda: body(xs_ref))


def _resid_norm(xp, xs, y, gain_post, gain_pre):
    m, d = y.shape
    tm, np_tiles, prompt, sample, stacked, vec = _split_rows(xp.shape[0], xs.shape[0], d, 256)
    return pl.pallas_call(
        functools.partial(_resid_norm_kernel, np_tiles=np_tiles),
        grid=(m // tm,),
        in_specs=[prompt, sample, stacked, vec, vec],
        out_specs=[stacked, stacked],
        out_shape=[jax.ShapeDtypeStruct((m, d), F32), jax.ShapeDtypeStruct((m, d), BF16)],
        compiler_params=_params("arbitrary"),
        name="resid_norm",
    )(xp, xs, y, gain_post.reshape(1, d), gain_pre.reshape(1, d))


def _resid_final_kernel(x_ref, y_ref, g_ref, op_ref, os_ref, *, np_tiles):
    i = pl.program_id(0)

    @pl.when(i < np_tiles)
    def _():
        op_ref[...] = x_ref[...] + _rms(y_ref[...], g_ref[...])

    @pl.when(i >= np_tiles)
    def _():
        os_ref[...] = x_ref[...] + _rms(y_ref[...], g_ref[...])


def _resid_final(x, y, gain, n_prompt):
    m, d = x.shape
    tm, np_tiles, prompt, sample, stacked, vec = _split_rows(n_prompt, m - n_prompt, d, 256)
    return pl.pallas_call(
        functools.partial(_resid_final_kernel, np_tiles=np_tiles),
        grid=(m // tm,),
        in_specs=[stacked, stacked, vec],
        out_specs=[prompt, sample],
        out_shape=[jax.ShapeDtypeStruct((n_prompt, d), F32),
                   jax.ShapeDtypeStruct((m - n_prompt, d), F32)],
        compiler_params=_params("arbitrary"),
        name="resid_final",
    )(x, y, gain.reshape(1, d))


def _buffer_update_step(step, nsteps, src_ref, new_ref, dst_ref, stage, sem):
    nseq, rows, _ = src_ref.shape
    shift = new_ref.shape[1]
    body = rows - shift
    every = nsteps // (nseq + 1)
    turn = step // every
    active = step % every == 0

    def load(b):
        return pltpu.make_async_copy(src_ref.at[b, pl.ds(shift, body)], stage.at[b % 2], sem.at[0, b % 2])

    def store(b):
        return pltpu.make_async_copy(stage.at[b % 2], dst_ref.at[b, pl.ds(0, body)], sem.at[1, b % 2])

    def tail(b):
        return pltpu.make_async_copy(new_ref.at[b], dst_ref.at[b, pl.ds(body, shift)], sem.at[2, b % 2])

    @pl.when(active & (turn == 0))
    def _():
        load(0).start(priority=1)

    @pl.when(active & (turn < nseq))
    def _():
        load(turn).wait()
        store(turn).start(priority=1)
        tail(turn).start(priority=1)

    @pl.when(active & (turn >= 1) & (turn <= nseq))
    def _():
        store(turn - 1).wait()
        tail(turn - 1).wait()

    @pl.when(active & (turn + 1 < nseq))
    def _():
        load(turn + 1).start(priority=1)


def _mm_kernel(*refs, act, update):
    if update:
        a_ref, w_ref, src_ref, new_ref, o_ref, dst_ref, stage, sem = refs
        step = pl.program_id(0) * pl.num_programs(1) + pl.program_id(1)
        _buffer_update_step(step, update, src_ref, new_ref, dst_ref, stage, sem)
    else:
        a_ref, w_ref, o_ref = refs
    acc = _dot(a_ref[...], w_ref[...])
    if act == "relu2":
        acc = jnp.maximum(acc, 0.0)
        acc = acc * acc
    o_ref[...] = acc.astype(o_ref.dtype)


def _mm_bias_sigmoid_kernel(a_ref, w_ref, b_ref, o_ref):
    o_ref[...] = _sigmoid(_dot(a_ref[...], w_ref[...]) + b_ref[...]).astype(o_ref.dtype)


def _update_args(update, nsteps):
    if update is None:
        return (), [], [], [], []
    src, new = update
    assert nsteps >= src.shape[0] + 1, "grid too short to finish the buffer update"
    any_spec = pl.BlockSpec(memory_space=pl.ANY)
    stage = pltpu.VMEM((2, src.shape[1] - new.shape[1], src.shape[2]), src.dtype)
    return ((src, new), [any_spec, any_spec], [any_spec],
            [jax.ShapeDtypeStruct(src.shape, src.dtype)], [stage, pltpu.SemaphoreType.DMA((3, 2))])


def _matmul(a, w, *, out_dtype=F32, act=None, bias=None, update=None, name="matmul"):
    m, k = a.shape
    n = w.shape[1]
    tm = _tile(m, 1024, SUBLANES)
    tn = _tile(n, 1024 if update is None else 512, LANES)
    grid = (m // tm, n // tn)
    a_spec = pl.BlockSpec((tm, k), lambda i, j: (i, 0))
    w_spec = pl.BlockSpec((k, tn), lambda i, j: (0, j))
    o_spec = pl.BlockSpec((tm, tn), lambda i, j: (i, j))
    u_in, u_specs, u_out_specs, u_shapes, scratch = _update_args(update, grid[0] * grid[1])
    if bias is None:
        body = functools.partial(_mm_kernel, act=act, update=update and grid[0] * grid[1])
        ins, specs = (a, w, *u_in), [a_spec, w_spec, *u_specs]
    else:
        assert update is None
        body, ins = _mm_bias_sigmoid_kernel, (a, w, bias.reshape(1, n))
        specs = [a_spec, w_spec, pl.BlockSpec((1, tn), lambda i, j: (0, j))]
    out = pl.pallas_call(
        body,
        grid=grid,
        in_specs=specs,
        out_specs=[o_spec, *u_out_specs],
        out_shape=[jax.ShapeDtypeStruct((m, n), out_dtype), *u_shapes],
        scratch_shapes=scratch,
        compiler_params=_params("arbitrary", "arbitrary"),
        name=name,
    )(*ins)
    return out if update is not None else out[0]


def _mm_acc_kernel(*refs, update):
    if update:
        a_ref, w_ref, src_ref, new_ref, o_ref, dst_ref, stage, sem = refs
        step = ((pl.program_id(0) * pl.num_programs(1) + pl.program_id(1)) * pl.num_programs(2)
                + pl.program_id(2))
        _buffer_update_step(step, update, src_ref, new_ref, dst_ref, stage, sem)
    else:
        a_ref, w_ref, o_ref = refs

    @pl.when(pl.program_id(2) == 0)
    def _():
        o_ref[...] = jnp.zeros_like(o_ref)

    o_ref[...] += _dot(a_ref[...], w_ref[...])


def _matmul_ksplit(a, w, *, tk, update=None, name):
    m, k = a.shape
    n = w.shape[1]
    tm = _tile(m, 1024, SUBLANES)
    tn = _tile(n, 1024, LANES)
    tk = _tile(k, tk, LANES)
    grid = (m // tm, n // tn, k // tk)
    u_in, u_specs, u_out_specs, u_shapes, scratch = _update_args(update, grid[0] * grid[1] * grid[2])
    out = pl.pallas_call(
        functools.partial(_mm_acc_kernel, update=update and grid[0] * grid[1] * grid[2]),
        grid=grid,
        in_specs=[pl.BlockSpec((tm, tk), lambda i, j, l: (i, l)),
                  pl.BlockSpec((tk, tn), lambda i, j, l: (l, j)), *u_specs],
        out_specs=[pl.BlockSpec((tm, tn), lambda i, j, l: (i, j)), *u_out_specs],
        out_shape=[jax.ShapeDtypeStruct((m, n), F32), *u_shapes],
        scratch_shapes=scratch,
        compiler_params=_params("arbitrary", "arbitrary", "arbitrary"),
        name=name,
    )(a, w, *u_in)
    return out if update is not None else out[0]


def _merge_kernel(att_ref, hg_ref, wa_ref, wh_ref, ga_ref, gh_ref, o_ref):
    pa = _dot(att_ref[...], wa_ref[...])
    ph = _dot(hg_ref[...], wh_ref[...])
    o_ref[...] = (ga_ref[...] * pa + gh_ref[...] * ph).astype(o_ref.dtype)


def _merge_proj(att, hg, w_att, w_hg, gates):
    m, ka = att.shape
    kh = hg.shape[1]
    d = w_att.shape[1]
    tm = _tile(m, 1024, SUBLANES)
    tn = _tile(d, 1024, LANES)
    nj = d // tn
    return pl.pallas_call(
        _merge_kernel,
        grid=(m // tm, nj),
        in_specs=[pl.BlockSpec((tm, ka), lambda i, j: (i, 0)),
                  pl.BlockSpec((tm, kh), lambda i, j: (i, 0)),
                  pl.BlockSpec((ka, tn), lambda i, j: (0, j)),
                  pl.BlockSpec((kh, tn), lambda i, j: (0, j)),
                  pl.BlockSpec((tm, tn), lambda i, j: (i, j)),
                  pl.BlockSpec((tm, tn), lambda i, j: (i, j + nj))],
        out_specs=pl.BlockSpec((tm, tn), lambda i, j: (i, j)),
        out_shape=jax.ShapeDtypeStruct((m, d), BF16),
        compiler_params=_params("parallel", "arbitrary"),
        name="merge_proj",
    )(att, hg, w_att, w_hg, gates, gates)


def _attn_prompt_kernel(q_ref, kp_ref, kc_ref, vp_ref, vc_ref, o_ref, lse_ref, *, dilation):
    jb = pl.program_id(1)
    blk = ATT_BLOCK
    r = lax.broadcasted_iota(jnp.int32, (blk, 2 * blk), 0)
    c = lax.broadcasted_iota(jnp.int32, (blk, 2 * blk), 1)
    dist = blk + r - c
    mask = (dist >= 0) & (dist <= blk) & ((c >= blk) | (jb > 0))
    for res in range(dilation):
        rows = pl.ds(res, blk, stride=dilation) if dilation > 1 else slice(None)
        for h in range(q_ref.shape[1] // ATT_HEAD_DIM):
            sl = slice(h * ATT_HEAD_DIM, (h + 1) * ATT_HEAD_DIM)
            q = q_ref[rows, sl].astype(BF16)
            k = jnp.concatenate([kp_ref[rows, sl], kc_ref[rows, sl]], axis=0).astype(BF16)
            v = jnp.concatenate([vp_ref[rows, sl], vc_ref[rows, sl]], axis=0).astype(BF16)
            s = jnp.where(mask, _dot_nt(q, k) * ATT_SCALE, NEG_BIG)
            m = jnp.max(s, axis=-1, keepdims=True)
            p = jnp.exp(s - m)
            l = jnp.sum(p, axis=-1, keepdims=True)
            o_ref[rows, sl] = _dot(p.astype(BF16), v) / l
            lse_ref[rows, sl] = jnp.broadcast_to(m + jnp.log(l), (blk, ATT_HEAD_DIM))


def _attn_prompt(proj, seq, group, dilation):
    chunk = ATT_BLOCK * dilation
    nch = seq // chunk
    lw = ATT_OUT_WIDTH if dilation == 1 else ATT_HEAD_DIM
    nhb = ATT_OUT_WIDTH // lw
    ngroups = len(ATT_GROUPS)

    def spec(col, prev):
        c0 = col * nhb
        if prev:
            return pl.BlockSpec((chunk, lw), lambda hb, j: (jnp.maximum(j - 1, 0), c0 + hb))
        return pl.BlockSpec((chunk, lw), lambda hb, j: (j, c0 + hb))

    out_spec = pl.BlockSpec((chunk, lw), lambda hb, j: (j, hb))
    shape = jax.ShapeDtypeStruct((seq, ATT_OUT_WIDTH), F32)
    return pl.pallas_call(
        functools.partial(_attn_prompt_kernel, dilation=dilation),
        grid=(nhb, nch),
        in_specs=[spec(group, False),
                  spec(ngroups + group, True), spec(ngroups + group, False),
                  spec(2 * ngroups + group, True), spec(2 * ngroups + group, False)],
        out_specs=[out_spec, out_spec],
        out_shape=[shape, shape],
        compiler_params=_params("parallel", "arbitrary"),
        name=f"attn_prompt_d{dilation}",
    )(proj, proj, proj, proj, proj)


def _attn_sample_kernel(*refs, dilation, t_new, update):
    if update:
        (q_ref, kn_ref, vn_ref, kb_ref, vb_ref, kn2_ref, vn2_ref, kx_ref, vx_ref,
         o_ref, lse_ref, ko_ref, vo_ref, m_sc, l_sc, acc_sc) = refs
    else:
        q_ref, kn_ref, vn_ref, kb_ref, vb_ref, o_ref, lse_ref, m_sc, l_sc, acc_sc = refs
    lt = pl.program_id(1)
    nlt = pl.num_programs(1)
    nh = ATT_SLOTS
    tl = kb_ref.shape[1] // nh
    shift = t_new * nh
    dmask = dilation - 1
    rows = nh * t_new

    def lanes(h):
        return slice(h * ATT_HEAD_DIM, (h + 1) * ATT_HEAD_DIM)

    def per_head(fn):
        return jnp.concatenate([fn(h) for h in range(nh)], axis=0)

    def head_rows(x, h):
        return x[h * t_new:(h + 1) * t_new]

    q16 = [q_ref[0, :, lanes(h)].astype(BF16) for h in range(nh)]

    @pl.when(lt == 0)
    def _():
        s = per_head(lambda h: _dot_nt(q16[h], kn_ref[0, :, lanes(h)].astype(BF16))) * ATT_SCALE
        t = lax.broadcasted_iota(jnp.int32, (rows, t_new), 0) % t_new
        i = lax.broadcasted_iota(jnp.int32, (rows, t_new), 1)
        s = jnp.where((i <= t) & (((t - i) & dmask) == 0), s, NEG_BIG)
        m = jnp.max(s, axis=-1, keepdims=True)
        p = jnp.exp(s - m)
        m_sc[...] = m
        l_sc[...] = jnp.sum(p, axis=-1, keepdims=True)
        acc_sc[...] = per_head(lambda h: _dot(head_rows(p, h).astype(BF16),
                                              vn_ref[0, :, lanes(h)].astype(BF16)))

    s = per_head(lambda h: _dot_nt(q16[h], kb_ref[0, pl.ds(h, tl, stride=nh), :].astype(BF16))) * ATT_SCALE
    t = lax.broadcasted_iota(jnp.int32, (rows, tl), 0) % t_new
    j = lax.broadcasted_iota(jnp.int32, (rows, tl), 1) + lt * tl
    s = jnp.where((j >= t) & (((j - t) & dmask) == 0), s, NEG_BIG)
    m_old = m_sc[...]
    m_new = jnp.maximum(m_old, jnp.max(s, axis=-1, keepdims=True))
    alpha = jnp.exp(m_old - m_new)
    p = jnp.exp(s - m_new)
    l_sc[...] = alpha * l_sc[...] + jnp.sum(p, axis=-1, keepdims=True)
    pv = per_head(lambda h: _dot(head_rows(p, h).astype(BF16),
                                 vb_ref[0, pl.ds(h, tl, stride=nh), :].astype(BF16)))
    acc_sc[...] = alpha * acc_sc[...] + pv
    m_sc[...] = m_new

    if update:
        keep = tl * nh - shift
        ko_ref[0, :keep] = kb_ref[0, shift:]
        vo_ref[0, :keep] = vb_ref[0, shift:]

        @pl.when(lt < nlt - 1)
        def _():
            ko_ref[0, keep:] = kx_ref[0]
            vo_ref[0, keep:] = vx_ref[0]

        @pl.when(lt == nlt - 1)
        def _():
            ko_ref[0, keep:] = kn2_ref[0]
            vo_ref[0, keep:] = vn2_ref[0]

    @pl.when(lt == nlt - 1)
    def _():
        l = l_sc[...]
        out = acc_sc[...] / l
        lse = m_sc[...] + jnp.log(l)
        for h in range(nh):
            o_ref[0, :, lanes(h)] = head_rows(out, h)
            lse_ref[0, :, lanes(h)] = jnp.broadcast_to(head_rows(lse, h), (t_new, ATT_HEAD_DIM))


def _new_rows(proj3, batch0, col):
    db = proj3.shape[0] - batch0
    c0 = col * ATT_OUT_WIDTH
    return proj3[batch0:, :, c0:c0 + ATT_OUT_WIDTH].reshape(db, proj3.shape[1] * ATT_SLOTS, ATT_HEAD_DIM)


def _attn_sample(proj3, batch0, group, dilation, k_buf, v_buf, *, update):
    db, rows_total, _ = k_buf.shape
    nh = ATT_SLOTS
    length = rows_total // nh
    t_new = proj3.shape[1]
    ngroups = len(ATT_GROUPS)
    tl = _tile(length, 1024 if update else 2048, SUBLANES)
    nlt = length // tl
    step = tl // t_new
    last = length // t_new - 1
    shift = t_new * nh

    def new_spec(col):
        return pl.BlockSpec((1, t_new, ATT_OUT_WIDTH), lambda b, l: (batch0 + b, 0, col))

    new2_spec = pl.BlockSpec((1, shift, ATT_HEAD_DIM), lambda b, l: (b, 0, 0))
    buf_spec = pl.BlockSpec((1, tl * nh, ATT_HEAD_DIM), lambda b, l: (b, l, 0))
    next_spec = pl.BlockSpec((1, shift, ATT_HEAD_DIM),
                             lambda b, l: (b, jnp.minimum((l + 1) * step, last), 0))
    small = pl.BlockSpec((1, t_new, ATT_OUT_WIDTH), lambda b, l: (b, 0, 0))
    small_shape = jax.ShapeDtypeStruct((db, t_new, ATT_OUT_WIDTH), F32)
    buf_shape = jax.ShapeDtypeStruct(k_buf.shape, F32)
    rows = nh * t_new
    ins = [proj3, proj3, proj3, k_buf, v_buf]
    in_specs = [new_spec(group), new_spec(ngroups + group), new_spec(2 * ngroups + group),
                buf_spec, buf_spec]
    out_specs, out_shape = [small, small], [small_shape, small_shape]
    if update:
        ins += [_new_rows(proj3, batch0, ngroups + group), _new_rows(proj3, batch0, 2 * ngroups + group),
                k_buf, v_buf]
        in_specs += [new2_spec, new2_spec, next_spec, next_spec]
        out_specs += [buf_spec, buf_spec]
        out_shape += [buf_shape, buf_shape]
    return pl.pallas_call(
        functools.partial(_attn_sample_kernel, dilation=dilation, t_new=t_new, update=update),
        grid=(db, nlt),
        in_specs=in_specs,
        out_specs=out_specs,
        out_shape=out_shape,
        scratch_shapes=[pltpu.VMEM((rows, 1), F32), pltpu.VMEM((rows, 1), F32),
                        pltpu.VMEM((rows, ATT_HEAD_DIM), F32)],
        compiler_params=_params("parallel", "arbitrary"),
        name=f"attn_sample_d{dilation}",
    )(*ins)


def _combine_kernel(o0, o1, o2, l0, l1, l2, out_ref):
    a, b, c = l0[...], l1[...], l2[...]
    m = jnp.maximum(jnp.maximum(a, b), c)
    ea, eb, ec = jnp.exp(a - m), jnp.exp(b - m), jnp.exp(c - m)
    num = ea * o0[...] + eb * o1[...] + ec * o2[...]
    out_ref[...] = (num / (ea + eb + ec)).astype(out_ref.dtype)


def _combine(outs, lses):
    m, w = outs[0].shape
    tm = _tile(m, 512, SUBLANES)
    spec = pl.BlockSpec((tm, w), lambda i: (i, 0))
    return pl.pallas_call(
        _combine_kernel,
        grid=(m // tm,),
        in_specs=[spec] * 6,
        out_specs=spec,
        out_shape=jax.ShapeDtypeStruct((m, w), BF16),
        compiler_params=_params("parallel"),
        name="attn_combine",
    )(*outs, *lses)


def _cumsum_rows(x):
    c = x.shape[0]
    r = lax.broadcasted_iota(jnp.int32, (c, c), 0)
    s = lax.broadcasted_iota(jnp.int32, (c, c), 1)
    tri = jnp.where(r >= s, 1.0, 0.0).astype(BF16)
    hi = x.astype(BF16)
    rem = x - hi.astype(F32)
    mid = rem.astype(BF16)
    lo = (rem - mid.astype(F32)).astype(BF16)
    return _dot(tri, hi) + _dot(tri, mid) + _dot(tri, lo)


def _col_bcast(w):
    hi = w.astype(BF16).astype(F32)
    mid = (w - hi).astype(BF16).astype(F32)
    lo = (w - hi - mid).astype(BF16).astype(F32)
    terms = jnp.concatenate([hi, mid, lo, jnp.zeros((SUBLANES - 3, LANES), F32)], axis=0).astype(BF16)
    ones = jnp.where(lax.broadcasted_iota(jnp.int32, (SUBLANES, LANES), 0) < 3, 1.0, 0.0).astype(BF16)
    return _dot_tn(terms, ones)


def _lower_bound(lb_ref):
    a = lb_ref[...]
    e = jnp.exp(a - jnp.max(a, axis=0, keepdims=True))
    return e[0:1] / jnp.sum(e, axis=0, keepdims=True)


def _hgrn_chunk(qg, fg, ig, gg, lb, gain, st):
    c = qg.shape[0]
    q = qg * _sigmoid(qg) * HG_SCALE
    forget = lb + (1.0 - lb) * _sigmoid(fg)
    k = 1.0 - forget
    v = ig
    b = _cumsum_rows(jnp.log(forget))
    v16 = v.astype(BF16)

    nsub = c // HG_SUB
    b3 = b.reshape(nsub, HG_SUB, LANES)
    q3 = q.reshape(nsub, HG_SUB, LANES)
    k3 = k.reshape(nsub, HG_SUB, LANES)
    pos = lax.broadcasted_iota(jnp.int32, (nsub, HG_SUB, LANES), 1)
    row = lax.broadcasted_iota(jnp.int32, (c, c), 0)
    col = lax.broadcasted_iota(jnp.int32, (c, c), 1)
    scores = jnp.zeros((c, c), F32)
    for s in range(HG_SUB):
        diff = jnp.where(pos >= s, b3 - b3[:, s:s + 1, :], NEG_BIG)
        w = jnp.sum(q3 * k3[:, s:s + 1, :] * jnp.exp(diff), axis=-1, keepdims=True)
        w = jnp.broadcast_to(w.reshape(c, 1), (c, c))
        scores = jnp.where(col == (row // HG_SUB) * HG_SUB + s, w, scores)

    width = HG_SUB
    while width < c:
        pair = 2 * width
        bm = jnp.concatenate(
            [jnp.broadcast_to(b[p * pair + width - 1:p * pair + width], (pair, LANES))
             for p in range(c // pair)], axis=0)
        e = jnp.exp(-jnp.abs(b - bm))
        right = (lax.broadcasted_iota(jnp.int32, (c, LANES), 0) // width) % 2 == 1
        ql = jnp.where(right, q * e, 0.0).astype(BF16)
        kl = jnp.where(right, 0.0, k * e).astype(BF16)
        a = _dot_nt(ql, kl)
        scores = jnp.where((row // pair == col // pair) & (row // width != col // width) & (row > col),
                           a, scores)
        width = pair

    o = _dot(scores.astype(BF16), v16) + _dot_nt((q * jnp.exp(b)).astype(BF16), st.astype(BF16))
    b_last = b[c - 1:c]
    st_new = st * jnp.exp(b_last) + _dot_tn(v16, (k * jnp.exp(b_last - b)).astype(BF16))
    out = _rms(o, gain) * (gg * _sigmoid(gg))
    return out, st_new


def _hgrn_prompt_kernel(q_ref, f_ref, i_ref, g_ref, lb_ref, gain_ref, o_ref, s_ref, st_sc):
    tb = pl.program_id(1)

    @pl.when(tb == 0)
    def _():
        st_sc[...] = jnp.zeros_like(st_sc)

    nheads = st_sc.shape[0]
    gain = gain_ref[...]
    lbs = [_lower_bound(lb_ref.at[:, h * LANES:(h + 1) * LANES]) for h in range(nheads)]
    sts = [st_sc[h] for h in range(nheads)]
    for ci in range(q_ref.shape[0] // HG_CHUNK):
        rs = slice(ci * HG_CHUNK, (ci + 1) * HG_CHUNK)
        for h in range(nheads):
            sl = slice(h * LANES, (h + 1) * LANES)
            out, sts[h] = _hgrn_chunk(q_ref[rs, sl], f_ref[rs, sl], i_ref[rs, sl], g_ref[rs, sl],
                                      lbs[h], gain, sts[h])
            o_ref[rs, sl] = out.astype(o_ref.dtype)
    for h in range(nheads):
        st_sc[h] = sts[h]

    @pl.when(tb == pl.num_programs(1) - 1)
    def _():
        for h in range(nheads):
            s_ref[h] = sts[h].T


def _hgrn_prompt(proj, seq, hg_lower_bound, hg_norm, hg_width):
    heads = hg_width // HG_EXPAND
    hw = HG_EXPAND
    assert 3 * ATT_WIDTH % hw == 0
    base = 3 * ATT_WIDTH // hw
    per = hg_width // hw
    hpb = hw // HG_EXPAND
    tb = _tile(seq, 512, HG_CHUNK)

    def col(which):
        return pl.BlockSpec((tb, hw), lambda g, t: (t, base + which * per + g))

    depth1 = hg_lower_bound.shape[0]
    return pl.pallas_call(
        _hgrn_prompt_kernel,
        grid=(per, seq // tb),
        in_specs=[col(0), col(1), col(2), col(3),
                  pl.BlockSpec((depth1, hw), lambda g, t: (0, g)),
                  pl.BlockSpec((1, LANES), lambda g, t: (0, 0))],
        out_specs=[pl.BlockSpec((tb, hw), lambda g, t: (t, g)),
                   pl.BlockSpec((hpb, HG_EXPAND, HG_HEAD_V), lambda g, t: (g, 0, 0))],
        out_shape=[jax.ShapeDtypeStruct((seq, hg_width), BF16),
                   jax.ShapeDtypeStruct((heads, HG_EXPAND, HG_HEAD_V), F32)],
        scratch_shapes=[pltpu.VMEM((hpb, HG_EXPAND, HG_HEAD_V), F32)],
        compiler_params=_params("parallel", "arbitrary"),
        name="hgrn_prompt",
    )(proj, proj, proj, proj, hg_lower_bound, hg_norm.reshape(1, LANES))


def _hgrn_sample_kernel(q_ref, f_ref, i_ref, g_ref, lb_ref, gain_ref, s_ref, o_ref, so_ref):
    nheads = s_ref.shape[1]
    c = q_ref.shape[1]
    qg, fg, v, gg = q_ref[0], f_ref[0], i_ref[0], g_ref[0]
    lb = _lower_bound(lb_ref)
    gain = gain_ref[...]
    q = qg * _sigmoid(qg) * HG_SCALE
    forget = lb + (1.0 - lb) * _sigmoid(fg)
    k = 1.0 - forget
    b = _cumsum_rows(jnp.log(forget))
    b_last = b[c - 1:c]
    qdec = (q * jnp.exp(b)).astype(BF16)
    kdec = (k * jnp.exp(b_last - b)).astype(BF16)
    decay = jnp.exp(b_last)
    v16 = v.astype(BF16)
    t = lax.broadcasted_iota(jnp.int32, b.shape, 0)
    pair = [q * k[s:s + 1] * jnp.exp(jnp.where(t >= s, b - b[s:s + 1], NEG_BIG)) for s in range(c)]
    outs = []
    for h in range(nheads):
        sl = slice(h * LANES, (h + 1) * LANES)
        st = s_ref[0, h]
        o = _dot(qdec[:, sl], st.astype(BF16))
        for s in range(c):
            o = o + jnp.sum(pair[s][:, sl], axis=-1, keepdims=True) * v[s:s + 1, sl]
        so_ref[0, h] = st * _col_bcast(decay[:, sl]) + _dot_tn(kdec[:, sl], v16[:, sl])
        outs.append(_rms(o, gain))
    o_ref[0] = (jnp.concatenate(outs, axis=1) * (gg * _sigmoid(gg))).astype(o_ref.dtype)


def _hgrn_sample(proj3, batch0, state, hg_lower_bound, hg_norm):
    db, heads = state.shape[:2]
    t_new = proj3.shape[1]
    hg_width = heads * HG_EXPAND
    cw = _tile(hg_width, ATT_OUT_WIDTH, LANES)
    assert 3 * ATT_WIDTH % cw == 0
    base = 3 * ATT_WIDTH // cw
    per = hg_width // cw
    hb = cw // HG_EXPAND

    def col(which):
        return pl.BlockSpec((1, t_new, cw), lambda b, c: (batch0 + b, 0, base + which * per + c))

    depth1 = hg_lower_bound.shape[0]
    st_spec = pl.BlockSpec((1, hb, HG_EXPAND, HG_HEAD_V), lambda b, c: (b, c, 0, 0))
    return pl.pallas_call(
        _hgrn_sample_kernel,
        grid=(db, per),
        in_specs=[col(0), col(1), col(2), col(3),
                  pl.BlockSpec((depth1, cw), lambda b, c: (0, c)),
                  pl.BlockSpec((1, LANES), lambda b, c: (0, 0)),
                  st_spec],
        out_specs=[pl.BlockSpec((1, t_new, cw), lambda b, c: (b, 0, c)), st_spec],
        out_shape=[jax.ShapeDtypeStruct((db, t_new, hg_width), BF16),
                   jax.ShapeDtypeStruct(state.shape, F32)],
        compiler_params=_params("parallel", "parallel"),
        name="hgrn_sample",
    )(proj3, proj3, proj3, proj3, hg_lower_bound, hg_norm.reshape(1, LANES), state)


def kernel(x_prompt, x_sample, cache_k_w128, cache_v_w128, cache_k_w512, cache_v_w512,
           cache_k_w2048, cache_v_w2048, state_hgrn, hg_lower_bound, w_in, w_gate, b_gate,
           w_proj_att, w_proj_hg, w_out, hg_norm, norm_mix_pre, norm_mix_post,
           norm_ffn_pre, norm_ffn_post, w_up, w_down):
    assert w_in.shape[0] == 1, "single-layer trunk"
    batch, seq, d_model = x_prompt.shape
    db, t_new, _ = x_sample.shape
    assert batch == 1
    hg_width = w_proj_hg.shape[1]
    n_prompt = batch * seq
    n_sample = db * t_new
    caches = (cache_k_w128, cache_v_w128, cache_k_w512, cache_v_w512, cache_k_w2048, cache_v_w2048)

    xp = x_prompt.reshape(n_prompt, d_model)
    xs = x_sample.reshape(n_sample, d_model)
    h = _norm_cast(xp, xs, norm_mix_pre[0])
    proj = _matmul(h, w_in[0].astype(BF16), name="in_proj")
    gates = _matmul(h, w_gate[0].astype(BF16), bias=b_gate[0], name="gate_proj")
    in_width = proj.shape[1]
    proj3 = proj.reshape((n_prompt + n_sample) // t_new, t_new, in_width)
    batch0 = n_prompt // t_new

    ngroups = len(ATT_GROUPS)
    mlp_steps = ((n_prompt + n_sample) // _tile(n_prompt + n_sample, 1024, SUBLANES)
                 * (w_up.shape[2] // _tile(w_up.shape[2], 1024, LANES)))
    pending = [None, None]
    outs_p, lses_p, outs_s, lses_s, new_kv_p, new_kv_s = [], [], [], [], [], []
    for g, (window, dilation) in enumerate(ATT_GROUPS):
        o, lse = _attn_prompt(proj, seq, g, dilation)
        outs_p.append(o)
        lses_p.append(lse)
        keep = min(window, seq)
        for part in (1, 2):
            c0 = part * ATT_WIDTH + g * ATT_OUT_WIDTH
            rows = proj[n_prompt - keep:n_prompt, c0:c0 + ATT_OUT_WIDTH]
            new_kv_p.append(rows.reshape(1, batch, keep, ATT_SLOTS, ATT_HEAD_DIM))
        k_buf, v_buf = caches[2 * g][0], caches[2 * g + 1][0]
        length = k_buf.shape[1]
        assert length == window and length == dilation * ATT_BLOCK
        k_buf = k_buf.reshape(db, length * ATT_SLOTS, ATT_HEAD_DIM)
        v_buf = v_buf.reshape(db, length * ATT_SLOTS, ATT_HEAD_DIM)
        deferred = db + 1 <= mlp_steps and g == ngroups - 1
        res = _attn_sample(proj3, batch0, g, dilation, k_buf, v_buf, update=not deferred)
        outs_s.append(res[0].reshape(n_sample, ATT_OUT_WIDTH))
        lses_s.append(res[1].reshape(n_sample, ATT_OUT_WIDTH))
        if deferred:
            pending = [(k_buf, _new_rows(proj3, batch0, ngroups + g)),
                       (v_buf, _new_rows(proj3, batch0, 2 * ngroups + g))]
        else:
            new_kv_s += list(res[2:])
    att = jnp.concatenate([_combine(outs_p, lses_p), _combine(outs_s, lses_s)], axis=0)

    hg_p, state_p = _hgrn_prompt(proj, seq, hg_lower_bound, hg_norm[0], hg_width)
    hg_s, state_s = _hgrn_sample(proj3, batch0, state_hgrn[0], hg_lower_bound, hg_norm[0])
    hg = jnp.concatenate([hg_p, hg_s.reshape(n_sample, hg_width)], axis=0)

    merged = _merge_proj(att, hg, w_proj_att[0].astype(BF16), w_proj_hg[0].astype(BF16), gates)
    mixed = _matmul(merged, w_out[0].astype(BF16), name="out_proj")
    x1, h2 = _resid_norm(xp, xs, mixed, norm_mix_post[0], norm_ffn_pre[0])

    u = _matmul(h2, w_up[0].astype(BF16), out_dtype=BF16, act="relu2", update=pending[0], name="ffn_up")
    if pending[0] is not None:
        u, k_last = u
    z = _matmul_ksplit(u, w_down[0].astype(BF16), tk=2048, update=pending[1], name="ffn_down")
    if pending[1] is not None:
        z, v_last = z
        new_kv_s += [k_last, v_last]
    y_prompt, y_sample = _resid_final(x1, z, norm_ffn_post[0], n_prompt)
    y_prompt = y_prompt.reshape(batch, seq, d_model)
    y_sample = y_sample.reshape(db, t_new, d_model)
    heads = hg_width // HG_EXPAND
    new_kv_s = [c.reshape(src.shape) for c, src in zip(new_kv_s, caches)]
    return (y_prompt, y_sample, *new_kv_p,
            state_p.reshape(1, batch, heads, HG_EXPAND, HG_HEAD_V),
            *new_kv_s, state_s[None])
```

```python
import functools
import math

import jax
import jax.numpy as jnp
from jax import lax
from jax.experimental import pallas as pl
from jax.experimental.pallas import tpu as pltpu

ATT_HEAD_DIM = 128
ATT_SLOTS = 8
ATT_GROUPS = ((128, 1), (512, 4), (2048, 16))
ATT_BLOCK = 128
ATT_OUT_WIDTH = ATT_SLOTS * ATT_HEAD_DIM
ATT_WIDTH = len(ATT_GROUPS) * ATT_OUT_WIDTH
ATT_SCALE = ATT_HEAD_DIM ** -0.5
HG_EXPAND = 128
HG_HEAD_V = 128
HG_SCALE = HG_EXPAND ** -0.5
HG_CHUNK = 64
HG_SUB = 8
RMS_EPS = 1e-6
NEG_BIG = -1e30

V7X_VMEM_BYTES = 64 * 1024 * 1024
VMEM_LIMIT = V7X_VMEM_BYTES - 8 * 1024 * 1024
LANES = 128
SUBLANES = 8

BF16 = jnp.bfloat16
F32 = jnp.float32


def _params(*sem):
    return pltpu.CompilerParams(dimension_semantics=sem, vmem_limit_bytes=VMEM_LIMIT)


def _tile(n, target, mult):
    if n <= target:
        return n
    t = (target // mult) * mult
    while t >= mult:
        if n % t == 0:
            return t
        t -= mult
    raise ValueError(f"no tile for {n}")


def _sigmoid(x):
    return 1.0 / (1.0 + jnp.exp(-x))


def _dot(a, b):
    return jnp.dot(a, b, preferred_element_type=F32)


def _dot_nt(a, b):
    return lax.dot_general(a, b, (((1,), (1,)), ((), ())), preferred_element_type=F32)


def _dot_tn(a, b):
    return lax.dot_general(a, b, (((0,), (0,)), ((), ())), preferred_element_type=F32)


def _rms(x, gain):
    return x * lax.rsqrt(jnp.mean(x * x, axis=-1, keepdims=True) + RMS_EPS) * gain


def _split_rows(n_prompt, n_sample, d, target):
    tm = _tile(math.gcd(n_prompt, n_sample), target, SUBLANES)
    np_tiles = n_prompt // tm
    prompt = pl.BlockSpec((tm, d), lambda i: (jnp.minimum(i, np_tiles - 1), 0))
    sample = pl.BlockSpec((tm, d), lambda i: (jnp.maximum(i - np_tiles, 0), 0))
    stacked = pl.BlockSpec((tm, d), lambda i: (i, 0))
    vec = pl.BlockSpec((1, d), lambda i: (0, 0))
    return tm, np_tiles, prompt, sample, stacked, vec


def _norm_cast_kernel(xp_ref, xs_ref, g_ref, o_ref, *, np_tiles):
    i = pl.program_id(0)

    @pl.when(i < np_tiles)
    def _():
        o_ref[...] = _rms(xp_ref[...], g_ref[...]).astype(o_ref.dtype)

    @pl.when(i >= np_tiles)
    def _():
        o_ref[...] = _rms(xs_ref[...], g_ref[...]).astype(o_ref.dtype)


def _norm_cast(xp, xs, gain):
    d = xp.shape[1]
    m = xp.shape[0] + xs.shape[0]
    tm, np_tiles, prompt, sample, stacked, vec = _split_rows(xp.shape[0], xs.shape[0], d, 256)
    return pl.pallas_call(
        functools.partial(_norm_cast_kernel, np_tiles=np_tiles),
        grid=(m // tm,),
        in_specs=[prompt, sample, vec],
        out_specs=stacked,
        out_shape=jax.ShapeDtypeStruct((m, d), BF16),
        compiler_params=_params("arbitrary"),
        name="norm_cast",
    )(xp, xs, gain.reshape(1, d))


def _resid_norm_kernel(xp_ref, xs_ref, y_ref, gpost_ref, gpre_ref, x1_ref, h_ref, *, np_tiles):
    i = pl.program_id(0)

    def body(x_ref):
        x1 = x_ref[...] + _rms(y_ref[...], gpost_ref[...])
        x1_ref[...] = x1
        h_ref[...] = _rms(x1, gpre_ref[...]).astype(h_ref.dtype)

    pl.when(i < np_tiles)(lambda: body(xp_ref))
    pl.when(i >= np_tiles)(lambda: body(xs_ref))


def _resid_norm(xp, xs, y, gain_post, gain_pre):
    m, d = y.shape
    tm, np_tiles, prompt, sample, stacked, vec = _split_rows(xp.shape[0], xs.shape[0], d, 256)
    return pl.pallas_call(
        functools.partial(_resid_norm_kernel, np_tiles=np_tiles),
        grid=(m // tm,),
        in_specs=[prompt, sample, stacked, vec, vec],
        out_specs=[stacked, stacked],
        out_shape=[jax.ShapeDtypeStruct((m, d), F32), jax.ShapeDtypeStruct((m, d), BF16)],
        compiler_params=_params("arbitrary"),
        name="resid_norm",
    )(xp, xs, y, gain_post.reshape(1, d), gain_pre.reshape(1, d))


def _resid_final_kernel(x_ref, y_ref, g_ref, op_ref, os_ref, *, np_tiles):
    i = pl.program_id(0)

    @pl.when(i < np_tiles)
    def _():
        op_ref[...] = x_ref[...] + _rms(y_ref[...], g_ref[...])

    @pl.when(i >= np_tiles)
    def _():
        os_ref[...] = x_ref[...] + _rms(y_ref[...], g_ref[...])


def _resid_final(x, y, gain, n_prompt):
    m, d = x.shape
    tm, np_tiles, prompt, sample, stacked, vec = _split_rows(n_prompt, m - n_prompt, d, 256)
    return pl.pallas_call(
        functools.partial(_resid_final_kernel, np_tiles=np_tiles),
        grid=(m // tm,),
        in_specs=[stacked, stacked, vec],
        out_specs=[prompt, sample],
        out_shape=[jax.ShapeDtypeStruct((n_prompt, d), F32),
                   jax.ShapeDtypeStruct((m - n_prompt, d), F32)],
        compiler_params=_params("arbitrary"),
        name="resid_final",
    )(x, y, gain.reshape(1, d))


def _buffer_update_step(step, nsteps, src_ref, new_ref, dst_ref, stage, sem):
    nseq, rows, _ = src_ref.shape
    shift = new_ref.shape[1]
    body = rows - shift
    every = nsteps // (nseq + 1)
    turn = step // every
    active = step % every == 0

    def load(b):
        return pltpu.make_async_copy(src_ref.at[b, pl.ds(shift, body)], stage.at[b % 2], sem.at[0, b % 2])

    def store(b):
        return pltpu.make_async_copy(stage.at[b % 2], dst_ref.at[b, pl.ds(0, body)], sem.at[1, b % 2])

    def tail(b):
        return pltpu.make_async_copy(new_ref.at[b], dst_ref.at[b, pl.ds(body, shift)], sem.at[2, b % 2])

    @pl.when(active & (turn == 0))
    def _():
        load(0).start(priority=1)

    @pl.when(active & (turn < nseq))
    def _():
        load(turn).wait()
        store(turn).start(priority=1)
        tail(turn).start(priority=1)

    @pl.when(active & (turn >= 1) & (turn <= nseq))
    def _():
        store(turn - 1).wait()
        tail(turn - 1).wait()

    @pl.when(active & (turn + 1 < nseq))
    def _():
        load(turn + 1).start(priority=1)


def _mm_kernel(*refs, act, update):
    if update:
        a_ref, w_ref, src_ref, new_ref, o_ref, dst_ref, stage, sem = refs
        step = pl.program_id(0) * pl.num_programs(1) + pl.program_id(1)
        _buffer_update_step(step, update, src_ref, new_ref, dst_ref, stage, sem)
    else:
        a_ref, w_ref, o_ref = refs
    acc = _dot(a_ref[...], w_ref[...])
    if act == "relu2":
        acc = jnp.maximum(acc, 0.0)
        acc = acc * acc
    o_ref[...] = acc.astype(o_ref.dtype)


def _mm_bias_sigmoid_kernel(a_ref, w_ref, b_ref, o_ref):
    o_ref[...] = _sigmoid(_dot(a_ref[...], w_ref[...]) + b_ref[...]).astype(o_ref.dtype)


def _update_args(update, nsteps):
    if update is None:
        return (), [], [], [], []
    src, new = update
    assert nsteps >= src.shape[0] + 1, "grid too short to finish the buffer update"
    any_spec = pl.BlockSpec(memory_space=pl.ANY)
    stage = pltpu.VMEM((2, src.shape[1] - new.shape[1], src.shape[2]), src.dtype)
    return ((src, new), [any_spec, any_spec], [any_spec],
            [jax.ShapeDtypeStruct(src.shape, src.dtype)], [stage, pltpu.SemaphoreType.DMA((3, 2))])


def _matmul(a, w, *, out_dtype=F32, act=None, bias=None, update=None, name="matmul"):
    m, k = a.shape
    n = w.shape[1]
    tm = _tile(m, 1024, SUBLANES)
    tn = _tile(n, 1024 if update is None else 512, LANES)
    grid = (m // tm, n // tn)
    a_spec = pl.BlockSpec((tm, k), lambda i, j: (i, 0))
    w_spec = pl.BlockSpec((k, tn), lambda i, j: (0, j))
    o_spec = pl.BlockSpec((tm, tn), lambda i, j: (i, j))
    u_in, u_specs, u_out_specs, u_shapes, scratch = _update_args(update, grid[0] * grid[1])
    if bias is None:
        body = functools.partial(_mm_kernel, act=act, update=update and grid[0] * grid[1])
        ins, specs = (a, w, *u_in), [a_spec, w_spec, *u_specs]
    else:
        assert update is None
        body, ins = _mm_bias_sigmoid_kernel, (a, w, bias.reshape(1, n))
        specs = [a_spec, w_spec, pl.BlockSpec((1, tn), lambda i, j: (0, j))]
    out = pl.pallas_call(
        body,
        grid=grid,
        in_specs=specs,
        out_specs=[o_spec, *u_out_specs],
        out_shape=[jax.ShapeDtypeStruct((m, n), out_dtype), *u_shapes],
        scratch_shapes=scratch,
        compiler_params=_params("arbitrary", "arbitrary"),
        name=name,
    )(*ins)
    return out if update is not None else out[0]


def _mm_acc_kernel(*refs, update):
    if update:
        a_ref, w_ref, src_ref, new_ref, o_ref, dst_ref, stage, sem = refs
        step = ((pl.program_id(0) * pl.num_programs(1) + pl.program_id(1)) * pl.num_programs(2)
                + pl.program_id(2))
        _buffer_update_step(step, update, src_ref, new_ref, dst_ref, stage, sem)
    else:
        a_ref, w_ref, o_ref = refs

    @pl.when(pl.program_id(2) == 0)
    def _():
        o_ref[...] = jnp.zeros_like(o_ref)

    o_ref[...] += _dot(a_ref[...], w_ref[...])


def _matmul_ksplit(a, w, *, tk, update=None, name):
    m, k = a.shape
    n = w.shape[1]
    tm = _tile(m, 1024, SUBLANES)
    tn = _tile(n, 1024, LANES)
    tk = _tile(k, tk, LANES)
    grid = (m // tm, n // tn, k // tk)
    u_in, u_specs, u_out_specs, u_shapes, scratch = _update_args(update, grid[0] * grid[1] * grid[2])
    out = pl.pallas_call(
        functools.partial(_mm_acc_kernel, update=update and grid[0] * grid[1] * grid[2]),
        grid=grid,
        in_specs=[pl.BlockSpec((tm, tk), lambda i, j, l: (i, l)),
                  pl.BlockSpec((tk, tn), lambda i, j, l: (l, j)), *u_specs],
        out_specs=[pl.BlockSpec((tm, tn), lambda i, j, l: (i, j)), *u_out_specs],
        out_shape=[jax.ShapeDtypeStruct((m, n), F32), *u_shapes],
        scratch_shapes=scratch,
        compiler_params=_params("arbitrary", "arbitrary", "arbitrary"),
        name=name,
    )(a, w, *u_in)
    return out if update is not None else out[0]


def _merge_kernel(att_ref, hg_ref, wa_ref, wh_ref, ga_ref, gh_ref, o_ref):
    pa = _dot(att_ref[...], wa_ref[...])
    ph = _dot(hg_ref[...], wh_ref[...])
    o_ref[...] = (ga_ref[...] * pa + gh_ref[...] * ph).astype(o_ref.dtype)


def _merge_proj(att, hg, w_att, w_hg, gates):
    m, ka = att.shape
    kh = hg.shape[1]
    d = w_att.shape[1]
    tm = _tile(m, 1024, SUBLANES)
    tn = _tile(d, 1024, LANES)
    nj = d // tn
    return pl.pallas_call(
        _merge_kernel,
        grid=(m // tm, nj),
        in_specs=[pl.BlockSpec((tm, ka), lambda i, j: (i, 0)),
                  pl.BlockSpec((tm, kh), lambda i, j: (i, 0)),
                  pl.BlockSpec((ka, tn), lambda i, j: (0, j)),
                  pl.BlockSpec((kh, tn), lambda i, j: (0, j)),
                  pl.BlockSpec((tm, tn), lambda i, j: (i, j)),
                  pl.BlockSpec((tm, tn), lambda i, j: (i, j + nj))],
        out_specs=pl.BlockSpec((tm, tn), lambda i, j: (i, j)),
        out_shape=jax.ShapeDtypeStruct((m, d), BF16),
        compiler_params=_params("parallel", "arbitrary"),
        name="merge_proj",
    )(att, hg, w_att, w_hg, gates, gates)


def _attn_prompt_kernel(q_ref, kp_ref, kc_ref, vp_ref, vc_ref, o_ref, lse_ref, *, dilation):
    jb = pl.program_id(1)
    blk = ATT_BLOCK
    r = lax.broadcasted_iota(jnp.int32, (blk, 2 * blk), 0)
    c = lax.broadcasted_iota(jnp.int32, (blk, 2 * blk), 1)
    dist = blk + r - c
    mask = (dist >= 0) & (dist <= blk) & ((c >= blk) | (jb > 0))
    for res in range(dilation):
        rows = pl.ds(res, blk, stride=dilation) if dilation > 1 else slice(None)
        for h in range(q_ref.shape[1] // ATT_HEAD_DIM):
            sl = slice(h * ATT_HEAD_DIM, (h + 1) * ATT_HEAD_DIM)
            q = q_ref[rows, sl].astype(BF16)
            k = jnp.concatenate([kp_ref[rows, sl], kc_ref[rows, sl]], axis=0).astype(BF16)
            v = jnp.concatenate([vp_ref[rows, sl], vc_ref[rows, sl]], axis=0).astype(BF16)
            s = jnp.where(mask, _dot_nt(q, k) * ATT_SCALE, NEG_BIG)
            m = jnp.max(s, axis=-1, keepdims=True)
            p = jnp.exp(s - m)
            l = jnp.sum(p, axis=-1, keepdims=True)
            o_ref[rows, sl] = _dot(p.astype(BF16), v) / l
            lse_ref[rows, sl] = jnp.broadcast_to(m + jnp.log(l), (blk, ATT_HEAD_DIM))


def _attn_prompt(proj, seq, group, dilation):
    chunk = ATT_BLOCK * dilation
    nch = seq // chunk
    lw = ATT_OUT_WIDTH if dilation == 1 else ATT_HEAD_DIM
    nhb = ATT_OUT_WIDTH // lw
    ngroups = len(ATT_GROUPS)

    def spec(col, prev):
        c0 = col * nhb
        if prev:
            return pl.BlockSpec((chunk, lw), lambda hb, j: (jnp.maximum(j - 1, 0), c0 + hb))
        return pl.BlockSpec((chunk, lw), lambda hb, j: (j, c0 + hb))

    out_spec = pl.BlockSpec((chunk, lw), lambda hb, j: (j, hb))
    shape = jax.ShapeDtypeStruct((seq, ATT_OUT_WIDTH), F32)
    return pl.pallas_call(
        functools.partial(_attn_prompt_kernel, dilation=dilation),
        grid=(nhb, nch),
        in_specs=[spec(group, False),
                  spec(ngroups + group, True), spec(ngroups + group, False),
                  spec(2 * ngroups + group, True), spec(2 * ngroups + group, False)],
        out_specs=[out_spec, out_spec],
        out_shape=[shape, shape],
        compiler_params=_params("parallel", "arbitrary"),
        name=f"attn_prompt_d{dilation}",
    )(proj, proj, proj, proj, proj)


def _attn_sample_kernel(*refs, dilation, t_new, update):
    if update:
        (q_ref, kn_ref, vn_ref, kb_ref, vb_ref, kn2_ref, vn2_ref, kx_ref, vx_ref,
         o_ref, lse_ref, ko_ref, vo_ref, m_sc, l_sc, acc_sc) = refs
    else:
        q_ref, kn_ref, vn_ref, kb_ref, vb_ref, o_ref, lse_ref, m_sc, l_sc, acc_sc = refs
    lt = pl.program_id(1)
    nlt = pl.num_programs(1)
    nh = ATT_SLOTS
    tl = kb_ref.shape[1] // nh
    shift = t_new * nh
    dmask = dilation - 1
    rows = nh * t_new

    def lanes(h):
        return slice(h * ATT_HEAD_DIM, (h + 1) * ATT_HEAD_DIM)

    def per_head(fn):
        return jnp.concatenate([fn(h) for h in range(nh)], axis=0)

    def head_rows(x, h):
        return x[h * t_new:(h + 1) * t_new]

    q16 = [q_ref[0, :, lanes(h)].astype(BF16) for h in range(nh)]

    @pl.when(lt == 0)
    def _():
        s = per_head(lambda h: _dot_nt(q16[h], kn_ref[0, :, lanes(h)].astype(BF16))) * ATT_SCALE
        t = lax.broadcasted_iota(jnp.int32, (rows, t_new), 0) % t_new
        i = lax.broadcasted_iota(jnp.int32, (rows, t_new), 1)
        s = jnp.where((i <= t) & (((t - i) & dmask) == 0), s, NEG_BIG)
        m = jnp.max(s, axis=-1, keepdims=True)
        p = jnp.exp(s - m)
        m_sc[...] = m
        l_sc[...] = jnp.sum(p, axis=-1, keepdims=True)
        acc_sc[...] = per_head(lambda h: _dot(head_rows(p, h).astype(BF16),
                                              vn_ref[0, :, lanes(h)].astype(BF16)))

    s = per_head(lambda h: _dot_nt(q16[h], kb_ref[0, pl.ds(h, tl, stride=nh), :].astype(BF16))) * ATT_SCALE
    t = lax.broadcasted_iota(jnp.int32, (rows, tl), 0) % t_new
    j = lax.broadcasted_iota(jnp.int32, (rows, tl), 1) + lt * tl
    s = jnp.where((j >= t) & (((j - t) & dmask) == 0), s, NEG_BIG)
    m_old = m_sc[...]
    m_new = jnp.maximum(m_old, jnp.max(s, axis=-1, keepdims=True))
    alpha = jnp.exp(m_old - m_new)
    p = jnp.exp(s - m_new)
    l_sc[...] = alpha * l_sc[...] + jnp.sum(p, axis=-1, keepdims=True)
    pv = per_head(lambda h: _dot(head_rows(p, h).astype(BF16),
                                 vb_ref[0, pl.ds(h, tl, stride=nh), :].astype(BF16)))
    acc_sc[...] = alpha * acc_sc[...] + pv
    m_sc[...] = m_new

    if update:
        keep = tl * nh - shift
        ko_ref[0, :keep] = kb_ref[0, shift:]
        vo_ref[0, :keep] = vb_ref[0, shift:]

        @pl.when(lt < nlt - 1)
        def _():
            ko_ref[0, keep:] = kx_ref[0]
            vo_ref[0, keep:] = vx_ref[0]

        @pl.when(lt == nlt - 1)
        def _():
            ko_ref[0, keep:] = kn2_ref[0]
            vo_ref[0, keep:] = vn2_ref[0]

    @pl.when(lt == nlt - 1)
    def _():
        l = l_sc[...]
        out = acc_sc[...] / l
        lse = m_sc[...] + jnp.log(l)
        for h in range(nh):
            o_ref[0, :, lanes(h)] = head_rows(out, h)
            lse_ref[0, :, lanes(h)] = jnp.broadcast_to(head_rows(lse, h), (t_new, ATT_HEAD_DIM))


def _new_rows(proj3, batch0, col):
    db = proj3.shape[0] - batch0
    c0 = col * ATT_OUT_WIDTH
    return proj3[batch0:, :, c0:c0 + ATT_OUT_WIDTH].reshape(db, proj3.shape[1] * ATT_SLOTS, ATT_HEAD_DIM)


def _attn_sample(proj3, batch0, group, dilation, k_buf, v_buf, *, update):
    db, rows_total, _ = k_buf.shape
    nh = ATT_SLOTS
    length = rows_total // nh
    t_new = proj3.shape[1]
    ngroups = len(ATT_GROUPS)
    tl = _tile(length, 1024 if update else 2048, SUBLANES)
    nlt = length // tl
    step = tl // t_new
    last = length // t_new - 1
    shift = t_new * nh

    def new_spec(col):
        return pl.BlockSpec((1, t_new, ATT_OUT_WIDTH), lambda b, l: (batch0 + b, 0, col))

    new2_spec = pl.BlockSpec((1, shift, ATT_HEAD_DIM), lambda b, l: (b, 0, 0))
    buf_spec = pl.BlockSpec((1, tl * nh, ATT_HEAD_DIM), lambda b, l: (b, l, 0))
    next_spec = pl.BlockSpec((1, shift, ATT_HEAD_DIM),
                             lambda b, l: (b, jnp.minimum((l + 1) * step, last), 0))
    small = pl.BlockSpec((1, t_new, ATT_OUT_WIDTH), lambda b, l: (b, 0, 0))
    small_shape = jax.ShapeDtypeStruct((db, t_new, ATT_OUT_WIDTH), F32)
    buf_shape = jax.ShapeDtypeStruct(k_buf.shape, F32)
    rows = nh * t_new
    ins = [proj3, proj3, proj3, k_buf, v_buf]
    in_specs = [new_spec(group), new_spec(ngroups + group), new_spec(2 * ngroups + group),
                buf_spec, buf_spec]
    out_specs, out_shape = [small, small], [small_shape, small_shape]
    if update:
        ins += [_new_rows(proj3, batch0, ngroups + group), _new_rows(proj3, batch0, 2 * ngroups + group),
                k_buf, v_buf]
        in_specs += [new2_spec, new2_spec, next_spec, next_spec]
        out_specs += [buf_spec, buf_spec]
        out_shape += [buf_shape, buf_shape]
    return pl.pallas_call(
        functools.partial(_attn_sample_kernel, dilation=dilation, t_new=t_new, update=update),
        grid=(db, nlt),
        in_specs=in_specs,
        out_specs=out_specs,
        out_shape=out_shape,
        scratch_shapes=[pltpu.VMEM((rows, 1), F32), pltpu.VMEM((rows, 1), F32),
                        pltpu.VMEM((rows, ATT_HEAD_DIM), F32)],
        compiler_params=_params("parallel", "arbitrary"),
        name=f"attn_sample_d{dilation}",
    )(*ins)


def _attn_sample_sparse_kernel(q_ref, kn_ref, vn_ref, kb_ref, vb_ref, o_ref, lse_ref, *, dilation, t_new):
    nh = ATT_SLOTS
    na = kb_ref.shape[1]
    rows = nh * t_new
    nkeys = na * t_new
    dmask = dilation - 1

    def lanes(h):
        return slice(h * ATT_HEAD_DIM, (h + 1) * ATT_HEAD_DIM)

    def per_head(fn):
        return jnp.concatenate([fn(h) for h in range(nh)], axis=0)

    def head_rows(x, h):
        return x[h * t_new:(h + 1) * t_new]

    def head_keys(ref, h):
        return ref[0, :, pl.ds(h, t_new, stride=nh), :].reshape(nkeys, ATT_HEAD_DIM).astype(BF16)

    q16 = [q_ref[0, :, lanes(h)].astype(BF16) for h in range(nh)]
    sn = per_head(lambda h: _dot_nt(q16[h], kn_ref[0, :, lanes(h)].astype(BF16))) * ATT_SCALE
    t = lax.broadcasted_iota(jnp.int32, (rows, t_new), 0) % t_new
    i = lax.broadcasted_iota(jnp.int32, (rows, t_new), 1)
    sn = jnp.where((i <= t) & (((t - i) & dmask) == 0), sn, NEG_BIG)
    sb = per_head(lambda h: _dot_nt(q16[h], head_keys(kb_ref, h))) * ATT_SCALE
    t = lax.broadcasted_iota(jnp.int32, (rows, nkeys), 0) % t_new
    c = lax.broadcasted_iota(jnp.int32, (rows, nkeys), 1) % t_new
    sb = jnp.where(c == t, sb, NEG_BIG)
    m = jnp.maximum(jnp.max(sn, axis=-1, keepdims=True), jnp.max(sb, axis=-1, keepdims=True))
    pn = jnp.exp(sn - m)
    pb = jnp.exp(sb - m)
    l = jnp.sum(pn, axis=-1, keepdims=True) + jnp.sum(pb, axis=-1, keepdims=True)
    acc = per_head(lambda h: _dot(head_rows(pn, h).astype(BF16), vn_ref[0, :, lanes(h)].astype(BF16))
                   + _dot(head_rows(pb, h).astype(BF16), head_keys(vb_ref, h)))
    out = acc / l
    lse = m + jnp.log(l)
    for h in range(nh):
        o_ref[0, :, lanes(h)] = head_rows(out, h)
        lse_ref[0, :, lanes(h)] = jnp.broadcast_to(head_rows(lse, h), (t_new, ATT_HEAD_DIM))


def _attn_sample_sparse(proj3, batch0, group, dilation, k_buf, v_buf):
    db, rows_total, _ = k_buf.shape
    nh = ATT_SLOTS
    t_new = proj3.shape[1]
    ngroups = len(ATT_GROUPS)
    na = rows_total // (dilation * nh)
    view = (db, na, dilation * nh, ATT_HEAD_DIM)

    def new_spec(col):
        return pl.BlockSpec((1, t_new, ATT_OUT_WIDTH), lambda b: (batch0 + b, 0, col))

    buf_spec = pl.BlockSpec((1, na, t_new * nh, ATT_HEAD_DIM), lambda b: (b, 0, 0, 0))
    small = pl.BlockSpec((1, t_new, ATT_OUT_WIDTH), lambda b: (b, 0, 0))
    small_shape = jax.ShapeDtypeStruct((db, t_new, ATT_OUT_WIDTH), F32)
    return pl.pallas_call(
        functools.partial(_attn_sample_sparse_kernel, dilation=dilation, t_new=t_new),
        grid=(db,),
        in_specs=[new_spec(group), new_spec(ngroups + group), new_spec(2 * ngroups + group),
                  buf_spec, buf_spec],
        out_specs=[small, small],
        out_shape=[small_shape, small_shape],
        compiler_params=_params("parallel"),
        name=f"attn_sample_d{dilation}",
    )(proj3, proj3, proj3, k_buf.reshape(view), v_buf.reshape(view))


def _combine_kernel(o0, o1, o2, l0, l1, l2, out_ref):
    a, b, c = l0[...], l1[...], l2[...]
    m = jnp.maximum(jnp.maximum(a, b), c)
    ea, eb, ec = jnp.exp(a - m), jnp.exp(b - m), jnp.exp(c - m)
    num = ea * o0[...] + eb * o1[...] + ec * o2[...]
    out_ref[...] = (num / (ea + eb + ec)).astype(out_ref.dtype)


def _combine(outs, lses):
    m, w = outs[0].shape
    tm = _tile(m, 512, SUBLANES)
    spec = pl.BlockSpec((tm, w), lambda i: (i, 0))
    return pl.pallas_call(
        _combine_kernel,
        grid=(m // tm,),
        in_specs=[spec] * 6,
        out_specs=spec,
        out_shape=jax.ShapeDtypeStruct((m, w), BF16),
        compiler_params=_params("parallel"),
        name="attn_combine",
    )(*outs, *lses)


def _cumsum_rows(x):
    c = x.shape[0]
    r = lax.broadcasted_iota(jnp.int32, (c, c), 0)
    s = lax.broadcasted_iota(jnp.int32, (c, c), 1)
    tri = jnp.where(r >= s, 1.0, 0.0).astype(BF16)
    hi = x.astype(BF16)
    rem = x - hi.astype(F32)
    mid = rem.astype(BF16)
    lo = (rem - mid.astype(F32)).astype(BF16)
    return _dot(tri, hi) + _dot(tri, mid) + _dot(tri, lo)


def _col_bcast(w):
    hi = w.astype(BF16).astype(F32)
    mid = (w - hi).astype(BF16).astype(F32)
    lo = (w - hi - mid).astype(BF16).astype(F32)
    terms = jnp.concatenate([hi, mid, lo, jnp.zeros((SUBLANES - 3, LANES), F32)], axis=0).astype(BF16)
    ones = jnp.where(lax.broadcasted_iota(jnp.int32, (SUBLANES, LANES), 0) < 3, 1.0, 0.0).astype(BF16)
    return _dot_tn(terms, ones)


def _lower_bound(lb_ref):
    a = lb_ref[...]
    e = jnp.exp(a - jnp.max(a, axis=0, keepdims=True))
    return e[0:1] / jnp.sum(e, axis=0, keepdims=True)


def _hgrn_chunk(qg, fg, ig, gg, lb, gain, st):
    c = qg.shape[0]
    q = qg * _sigmoid(qg) * HG_SCALE
    forget = lb + (1.0 - lb) * _sigmoid(fg)
    k = 1.0 - forget
    v = ig
    b = _cumsum_rows(jnp.log(forget))
    v16 = v.astype(BF16)

    nsub = c // HG_SUB
    b3 = b.reshape(nsub, HG_SUB, LANES)
    q3 = q.reshape(nsub, HG_SUB, LANES)
    k3 = k.reshape(nsub, HG_SUB, LANES)
    pos = lax.broadcasted_iota(jnp.int32, (nsub, HG_SUB, LANES), 1)
    row = lax.broadcasted_iota(jnp.int32, (c, c), 0)
    col = lax.broadcasted_iota(jnp.int32, (c, c), 1)
    scores = jnp.zeros((c, c), F32)
    for s in range(HG_SUB):
        diff = jnp.where(pos >= s, b3 - b3[:, s:s + 1, :], NEG_BIG)
        w = jnp.sum(q3 * k3[:, s:s + 1, :] * jnp.exp(diff), axis=-1, keepdims=True)
        w = jnp.broadcast_to(w.reshape(c, 1), (c, c))
        scores = jnp.where(col == (row // HG_SUB) * HG_SUB + s, w, scores)

    width = HG_SUB
    while width < c:
        pair = 2 * width
        bm = jnp.concatenate(
            [jnp.broadcast_to(b[p * pair + width - 1:p * pair + width], (pair, LANES))
             for p in range(c // pair)], axis=0)
        e = jnp.exp(-jnp.abs(b - bm))
        right = (lax.broadcasted_iota(jnp.int32, (c, LANES), 0) // width) % 2 == 1
        ql = jnp.where(right, q * e, 0.0).astype(BF16)
        kl = jnp.where(right, 0.0, k * e).astype(BF16)
        a = _dot_nt(ql, kl)
        scores = jnp.where((row // pair == col // pair) & (row // width != col // width) & (row > col),
                           a, scores)
        width = pair

    o = _dot(scores.astype(BF16), v16) + _dot_nt((q * jnp.exp(b)).astype(BF16), st.astype(BF16))
    b_last = b[c - 1:c]
    st_new = st * jnp.exp(b_last) + _dot_tn(v16, (k * jnp.exp(b_last - b)).astype(BF16))
    out = _rms(o, gain) * (gg * _sigmoid(gg))
    return out, st_new


def _hgrn_prompt_kernel(q_ref, f_ref, i_ref, g_ref, lb_ref, gain_ref, o_ref, s_ref, st_sc):
    tb = pl.program_id(1)

    @pl.when(tb == 0)
    def _():
        st_sc[...] = jnp.zeros_like(st_sc)

    nheads = st_sc.shape[0]
    gain = gain_ref[...]
    lbs = [_lower_bound(lb_ref.at[:, h * LANES:(h + 1) * LANES]) for h in range(nheads)]
    sts = [st_sc[h] for h in range(nheads)]
    for ci in range(q_ref.shape[0] // HG_CHUNK):
        rs = slice(ci * HG_CHUNK, (ci + 1) * HG_CHUNK)
        for h in range(nheads):
            sl = slice(h * LANES, (h + 1) * LANES)
            out, sts[h] = _hgrn_chunk(q_ref[rs, sl], f_ref[rs, sl], i_ref[rs, sl], g_ref[rs, sl],
                                      lbs[h], gain, sts[h])
            o_ref[rs, sl] = out.astype(o_ref.dtype)
    for h in range(nheads):
        st_sc[h] = sts[h]

    @pl.when(tb == pl.num_programs(1) - 1)
    def _():
        for h in range(nheads):
            s_ref[h] = sts[h].T


def _hgrn_prompt(proj, seq, hg_lower_bound, hg_norm, hg_width):
    heads = hg_width // HG_EXPAND
    hw = HG_EXPAND
    assert 3 * ATT_WIDTH % hw == 0
    base = 3 * ATT_WIDTH // hw
    per = hg_width // hw
    hpb = hw // HG_EXPAND
    tb = _tile(seq, 512, HG_CHUNK)

    def col(which):
        return pl.BlockSpec((tb, hw), lambda g, t: (t, base + which * per + g))

    depth1 = hg_lower_bound.shape[0]
    return pl.pallas_call(
        _hgrn_prompt_kernel,
        grid=(per, seq // tb),
        in_specs=[col(0), col(1), col(2), col(3),
                  pl.BlockSpec((depth1, hw), lambda g, t: (0, g)),
                  pl.BlockSpec((1, LANES), lambda g, t: (0, 0))],
        out_specs=[pl.BlockSpec((tb, hw), lambda g, t: (t, g)),
                   pl.BlockSpec((hpb, HG_EXPAND, HG_HEAD_V), lambda g, t: (g, 0, 0))],
        out_shape=[jax.ShapeDtypeStruct((seq, hg_width), BF16),
                   jax.ShapeDtypeStruct((heads, HG_EXPAND, HG_HEAD_V), F32)],
        scratch_shapes=[pltpu.VMEM((hpb, HG_EXPAND, HG_HEAD_V), F32)],
        compiler_params=_params("parallel", "arbitrary"),
        name="hgrn_prompt",
    )(proj, proj, proj, proj, hg_lower_bound, hg_norm.reshape(1, LANES))


def _hgrn_sample_kernel(q_ref, f_ref, i_ref, g_ref, lb_ref, gain_ref, s_ref, o_ref, so_ref):
    nheads = s_ref.shape[1]
    c = q_ref.shape[1]
    qg, fg, v, gg = q_ref[0], f_ref[0], i_ref[0], g_ref[0]
    lb = _lower_bound(lb_ref)
    gain = gain_ref[...]
    q = qg * _sigmoid(qg) * HG_SCALE
    forget = lb + (1.0 - lb) * _sigmoid(fg)
    k = 1.0 - forget
    b = _cumsum_rows(jnp.log(forget))
    b_last = b[c - 1:c]
    qdec = (q * jnp.exp(b)).astype(BF16)
    kdec = (k * jnp.exp(b_last - b)).astype(BF16)
    decay = jnp.exp(b_last)
    v16 = v.astype(BF16)
    t = lax.broadcasted_iota(jnp.int32, b.shape, 0)
    pair = [q * k[s:s + 1] * jnp.exp(jnp.where(t >= s, b - b[s:s + 1], NEG_BIG)) for s in range(c)]
    outs = []
    for h in range(nheads):
        sl = slice(h * LANES, (h + 1) * LANES)
        st = s_ref[0, h]
        o = _dot(qdec[:, sl], st.astype(BF16))
        for s in range(c):
            o = o + jnp.sum(pair[s][:, sl], axis=-1, keepdims=True) * v[s:s + 1, sl]
        so_ref[0, h] = st * _col_bcast(decay[:, sl]) + _dot_tn(kdec[:, sl], v16[:, sl])
        outs.append(_rms(o, gain))
    o_ref[0] = (jnp.concatenate(outs, axis=1) * (gg * _sigmoid(gg))).astype(o_ref.dtype)


def _hgrn_sample(proj3, batch0, state, hg_lower_bound, hg_norm):
    db, heads = state.shape[:2]
    t_new = proj3.shape[1]
    hg_width = heads * HG_EXPAND
    cw = _tile(hg_width, ATT_OUT_WIDTH, LANES)
    assert 3 * ATT_WIDTH % cw == 0
    base = 3 * ATT_WIDTH // cw
    per = hg_width // cw
    hb = cw // HG_EXPAND

    def col(which):
        return pl.BlockSpec((1, t_new, cw), lambda b, c: (batch0 + b, 0, base + which * per + c))

    depth1 = hg_lower_bound.shape[0]
    st_spec = pl.BlockSpec((1, hb, HG_EXPAND, HG_HEAD_V), lambda b, c: (b, c, 0, 0))
    return pl.pallas_call(
        _hgrn_sample_kernel,
        grid=(db, per),
        in_specs=[col(0), col(1), col(2), col(3),
                  pl.BlockSpec((depth1, cw), lambda b, c: (0, c)),
                  pl.BlockSpec((1, LANES), lambda b, c: (0, 0)),
                  st_spec],
        out_specs=[pl.BlockSpec((1, t_new, cw), lambda b, c: (b, 0, c)), st_spec],
        out_shape=[jax.ShapeDtypeStruct((db, t_new, hg_width), BF16),
                   jax.ShapeDtypeStruct(state.shape, F32)],
        compiler_params=_params("parallel", "parallel"),
        name="hgrn_sample",
    )(proj3, proj3, proj3, proj3, hg_lower_bound, hg_norm.reshape(1, LANES), state)


def kernel(x_prompt, x_sample, cache_k_w128, cache_v_w128, cache_k_w512, cache_v_w512,
           cache_k_w2048, cache_v_w2048, state_hgrn, hg_lower_bound, w_in, w_gate, b_gate,
           w_proj_att, w_proj_hg, w_out, hg_norm, norm_mix_pre, norm_mix_post,
           norm_ffn_pre, norm_ffn_post, w_up, w_down):
    assert w_in.shape[0] == 1, "single-layer trunk"
    batch, seq, d_model = x_prompt.shape
    db, t_new, _ = x_sample.shape
    assert batch == 1
    hg_width = w_proj_hg.shape[1]
    n_prompt = batch * seq
    n_sample = db * t_new
    caches = (cache_k_w128, cache_v_w128, cache_k_w512, cache_v_w512, cache_k_w2048, cache_v_w2048)

    xp = x_prompt.reshape(n_prompt, d_model)
    xs = x_sample.reshape(n_sample, d_model)
    h = _norm_cast(xp, xs, norm_mix_pre[0])
    proj = _matmul(h, w_in[0].astype(BF16), name="in_proj")
    gates = _matmul(h, w_gate[0].astype(BF16), bias=b_gate[0], name="gate_proj")
    in_width = proj.shape[1]
    proj3 = proj.reshape((n_prompt + n_sample) // t_new, t_new, in_width)
    batch0 = n_prompt // t_new

    ngroups = len(ATT_GROUPS)
    mlp_steps = ((n_prompt + n_sample) // _tile(n_prompt + n_sample, 1024, SUBLANES)
                 * (w_up.shape[2] // _tile(w_up.shape[2], 1024, LANES)))
    pending = [None, None]
    outs_p, lses_p, outs_s, lses_s, new_kv_p, new_kv_s = [], [], [], [], [], []
    for g, (window, dilation) in enumerate(ATT_GROUPS):
        o, lse = _attn_prompt(proj, seq, g, dilation)
        outs_p.append(o)
        lses_p.append(lse)
        keep = min(window, seq)
        for part in (1, 2):
            c0 = part * ATT_WIDTH + g * ATT_OUT_WIDTH
            rows = proj[n_prompt - keep:n_prompt, c0:c0 + ATT_OUT_WIDTH]
            new_kv_p.append(rows.reshape(1, batch, keep, ATT_SLOTS, ATT_HEAD_DIM))
        k_buf, v_buf = caches[2 * g][0], caches[2 * g + 1][0]
        length = k_buf.shape[1]
        assert length == window and length == dilation * ATT_BLOCK
        k_buf = k_buf.reshape(db, length * ATT_SLOTS, ATT_HEAD_DIM)
        v_buf = v_buf.reshape(db, length * ATT_SLOTS, ATT_HEAD_DIM)
        deferred = db + 1 <= mlp_steps and g == ngroups - 1
        if deferred and dilation >= t_new:
            res = _attn_sample_sparse(proj3, batch0, g, dilation, k_buf, v_buf)
        else:
            res = _attn_sample(proj3, batch0, g, dilation, k_buf, v_buf, update=not deferred)
        outs_s.append(res[0].reshape(n_sample, ATT_OUT_WIDTH))
        lses_s.append(res[1].reshape(n_sample, ATT_OUT_WIDTH))
        if deferred:
            pending = [(k_buf, _new_rows(proj3, batch0, ngroups + g)),
                       (v_buf, _new_rows(proj3, batch0, 2 * ngroups + g))]
        else:
            new_kv_s += list(res[2:])
    att = jnp.concatenate([_combine(outs_p, lses_p), _combine(outs_s, lses_s)], axis=0)

    hg_p, state_p = _hgrn_prompt(proj, seq, hg_lower_bound, hg_norm[0], hg_width)
    hg_s, state_s = _hgrn_sample(proj3, batch0, state_hgrn[0], hg_lower_bound, hg_norm[0])
    hg = jnp.concatenate([hg_p, hg_s.reshape(n_sample, hg_width)], axis=0)

    merged = _merge_proj(att, hg, w_proj_att[0].astype(BF16), w_proj_hg[0].astype(BF16), gates)
    mixed = _matmul(merged, w_out[0].astype(BF16), name="out_proj")
    x1, h2 = _resid_norm(xp, xs, mixed, norm_mix_post[0], norm_ffn_pre[0])

    u = _matmul(h2, w_up[0].astype(BF16), out_dtype=BF16, act="relu2", update=pending[0], name="ffn_up")
    if pending[0] is not None:
        u, k_last = u
    z = _matmul_ksplit(u, w_down[0].astype(BF16), tk=2048, update=pending[1], name="ffn_down")
    if pending[1] is not None:
        z, v_last = z
        new_kv_s += [k_last, v_last]
    y_prompt, y_sample = _resid_final(x1, z, norm_ffn_post[0], n_prompt)
    y_prompt = y_prompt.reshape(batch, seq, d_model)
    y_sample = y_sample.reshape(db, t_new, d_model)
    heads = hg_width // HG_EXPAND
    new_kv_s = [c.reshape(src.shape) for c, src in zip(new_kv_s, caches)]
    return (y_prompt, y_sample, *new_kv_p,
            state_p.reshape(1, batch, heads, HG_EXPAND, HG_HEAD_V),
            *new_kv_s, state_s[None])
```

```python
import functools
import math

import jax
import jax.numpy as jnp
from jax import lax
from jax.experimental import pallas as pl
from jax.experimental.pallas import tpu as pltpu

ATT_HEAD_DIM = 128
ATT_SLOTS = 8
ATT_GROUPS = ((128, 1), (512, 4), (2048, 16))
ATT_BLOCK = 128
ATT_OUT_WIDTH = ATT_SLOTS * ATT_HEAD_DIM
ATT_WIDTH = len(ATT_GROUPS) * ATT_OUT_WIDTH
ATT_SCALE = ATT_HEAD_DIM ** -0.5
HG_EXPAND = 128
HG_HEAD_V = 128
HG_SCALE = HG_EXPAND ** -0.5
HG_CHUNK = 64
HG_SUB = 8
RMS_EPS = 1e-6
NEG_BIG = -1e30

V7X_VMEM_BYTES = 64 * 1024 * 1024
VMEM_LIMIT = V7X_VMEM_BYTES - 8 * 1024 * 1024
LANES = 128
SUBLANES = 8

BF16 = jnp.bfloat16
F32 = jnp.float32


def _params(*sem):
    return pltpu.CompilerParams(dimension_semantics=sem, vmem_limit_bytes=VMEM_LIMIT)


def _tile(n, target, mult):
    if n <= target:
        return n
    t = (target // mult) * mult
    while t >= mult:
        if n % t == 0:
            return t
        t -= mult
    raise ValueError(f"no tile for {n}")


def _sigmoid(x):
    return 1.0 / (1.0 + jnp.exp(-x))


def _dot(a, b):
    return jnp.dot(a, b, preferred_element_type=F32)


def _dot_nt(a, b):
    return lax.dot_general(a, b, (((1,), (1,)), ((), ())), preferred_element_type=F32)


def _dot_tn(a, b):
    return lax.dot_general(a, b, (((0,), (0,)), ((), ())), preferred_element_type=F32)


def _rms(x, gain):
    return x * lax.rsqrt(jnp.mean(x * x, axis=-1, keepdims=True) + RMS_EPS) * gain


def _split_rows(n_prompt, n_sample, d, target):
    tm = _tile(math.gcd(n_prompt, n_sample), target, SUBLANES)
    np_tiles = n_prompt // tm
    prompt = pl.BlockSpec((tm, d), lambda i: (jnp.minimum(i, np_tiles - 1), 0))
    sample = pl.BlockSpec((tm, d), lambda i: (jnp.maximum(i - np_tiles, 0), 0))
    stacked = pl.BlockSpec((tm, d), lambda i: (i, 0))
    vec = pl.BlockSpec((1, d), lambda i: (0, 0))
    return tm, np_tiles, prompt, sample, stacked, vec


def _norm_cast_kernel(xp_ref, xs_ref, g_ref, o_ref, *, np_tiles):
    i = pl.program_id(0)

    @pl.when(i < np_tiles)
    def _():
        o_ref[...] = _rms(xp_ref[...], g_ref[...]).astype(o_ref.dtype)

    @pl.when(i >= np_tiles)
    def _():
        o_ref[...] = _rms(xs_ref[...], g_ref[...]).astype(o_ref.dtype)


def _norm_cast(xp, xs, gain):
    d = xp.shape[1]
    m = xp.shape[0] + xs.shape[0]
    tm, np_tiles, prompt, sample, stacked, vec = _split_rows(xp.shape[0], xs.shape[0], d, 256)
    return pl.pallas_call(
        functools.partial(_norm_cast_kernel, np_tiles=np_tiles),
        grid=(m // tm,),
        in_specs=[prompt, sample, vec],
        out_specs=stacked,
        out_shape=jax.ShapeDtypeStruct((m, d), BF16),
        compiler_params=_params("arbitrary"),
        name="norm_cast",
    )(xp, xs, gain.reshape(1, d))


def _resid_norm_kernel(xp_ref, xs_ref, y_ref, gpost_ref, gpre_ref, x1_ref, h_ref, *, np_tiles):
    i = pl.program_id(0)

    def body(x_ref):
        x1 = x_ref[...] + _rms(y_ref[...], gpost_ref[...])
        x1_ref[...] = x1
        h_ref[...] = _rms(x1, gpre_ref[...]).astype(h_ref.dtype)

    pl.when(i < np_tiles)(lambda: body(xp_ref))
    pl.when(i >= np_tiles)(lambda: body(xs_ref))


def _resid_norm(xp, xs, y, gain_post, gain_pre):
    m, d = y.shape
    tm, np_tiles, prompt, sample, stacked, vec = _split_rows(xp.shape[0], xs.shape[0], d, 256)
    return pl.pallas_call(
        functools.partial(_resid_norm_kernel, np_tiles=np_tiles),
        grid=(m // tm,),
        in_specs=[prompt, sample, stacked, vec, vec],
        out_specs=[stacked, stacked],
        out_shape=[jax.ShapeDtypeStruct((m, d), F32), jax.ShapeDtypeStruct((m, d), BF16)],
        compiler_params=_params("arbitrary"),
        name="resid_norm",
    )(xp, xs, y, gain_post.reshape(1, d), gain_pre.reshape(1, d))


def _resid_final_kernel(x_ref, y_ref, g_ref, op_ref, os_ref, *, np_tiles):
    i = pl.program_id(0)

    @pl.when(i < np_tiles)
    def _():
        op_ref[...] = x_ref[...] + _rms(y_ref[...], g_ref[...])

    @pl.when(i >= np_tiles)
    def _():
        os_ref[...] = x_ref[...] + _rms(y_ref[...], g_ref[...])


def _resid_final(x, y, gain, n_prompt):
    m, d = x.shape
    tm, np_tiles, prompt, sample, stacked, vec = _split_rows(n_prompt, m - n_prompt, d, 256)
    return pl.pallas_call(
        functools.partial(_resid_final_kernel, np_tiles=np_tiles),
        grid=(m // tm,),
        in_specs=[stacked, stacked, vec],
        out_specs=[prompt, sample],
        out_shape=[jax.ShapeDtypeStruct((n_prompt, d), F32),
                   jax.ShapeDtypeStruct((m - n_prompt, d), F32)],
        compiler_params=_params("arbitrary"),
        name="resid_final",
    )(x, y, gain.reshape(1, d))


def _buffer_update_step(step, nsteps, src_ref, new_ref, dst_ref, stage, sem):
    nseq, rows, _ = src_ref.shape
    shift = new_ref.shape[1]
    body = rows - shift
    every = nsteps // (nseq + 1)
    turn = step // every
    active = step % every == 0

    def load(b):
        return pltpu.make_async_copy(src_ref.at[b, pl.ds(shift, body)], stage.at[b % 2], sem.at[0, b % 2])

    def store(b):
        return pltpu.make_async_copy(stage.at[b % 2], dst_ref.at[b, pl.ds(0, body)], sem.at[1, b % 2])

    def tail(b):
        return pltpu.make_async_copy(new_ref.at[b], dst_ref.at[b, pl.ds(body, shift)], sem.at[2, b % 2])

    @pl.when(active & (turn == 0))
    def _():
        load(0).start(priority=1)

    @pl.when(active & (turn < nseq))
    def _():
        load(turn).wait()
        store(turn).start(priority=1)
        tail(turn).start(priority=1)

    @pl.when(active & (turn >= 1) & (turn <= nseq))
    def _():
        store(turn - 1).wait()
        tail(turn - 1).wait()

    @pl.when(active & (turn + 1 < nseq))
    def _():
        load(turn + 1).start(priority=1)


def _mm_kernel(*refs, act, update):
    if update:
        a_ref, w_ref, src_ref, new_ref, o_ref, dst_ref, stage, sem = refs
        step = pl.program_id(0) * pl.num_programs(1) + pl.program_id(1)
        _buffer_update_step(step, update, src_ref, new_ref, dst_ref, stage, sem)
    else:
        a_ref, w_ref, o_ref = refs
    acc = _dot(a_ref[...], w_ref[...])
    if act == "relu2":
        acc = jnp.maximum(acc, 0.0)
        acc = acc * acc
    o_ref[...] = acc.astype(o_ref.dtype)


def _mm_bias_sigmoid_kernel(a_ref, w_ref, b_ref, o_ref):
    o_ref[...] = _sigmoid(_dot(a_ref[...], w_ref[...]) + b_ref[...]).astype(o_ref.dtype)


def _update_args(update, nsteps):
    if update is None:
        return (), [], [], [], []
    src, new = update
    assert nsteps >= src.shape[0] + 1, "grid too short to finish the buffer update"
    any_spec = pl.BlockSpec(memory_space=pl.ANY)
    stage = pltpu.VMEM((2, src.shape[1] - new.shape[1], src.shape[2]), src.dtype)
    return ((src, new), [any_spec, any_spec], [any_spec],
            [jax.ShapeDtypeStruct(src.shape, src.dtype)], [stage, pltpu.SemaphoreType.DMA((3, 2))])


def _matmul(a, w, *, out_dtype=F32, act=None, bias=None, update=None, name="matmul"):
    m, k = a.shape
    n = w.shape[1]
    tm = _tile(m, 1024, SUBLANES)
    tn = _tile(n, 1024 if update is None else 512, LANES)
    grid = (m // tm, n // tn)
    a_spec = pl.BlockSpec((tm, k), lambda i, j: (i, 0))
    w_spec = pl.BlockSpec((k, tn), lambda i, j: (0, j))
    o_spec = pl.BlockSpec((tm, tn), lambda i, j: (i, j))
    u_in, u_specs, u_out_specs, u_shapes, scratch = _update_args(update, grid[0] * grid[1])
    if bias is None:
        body = functools.partial(_mm_kernel, act=act, update=update and grid[0] * grid[1])
        ins, specs = (a, w, *u_in), [a_spec, w_spec, *u_specs]
    else:
        assert update is None
        body, ins = _mm_bias_sigmoid_kernel, (a, w, bias.reshape(1, n))
        specs = [a_spec, w_spec, pl.BlockSpec((1, tn), lambda i, j: (0, j))]
    out = pl.pallas_call(
        body,
        grid=grid,
        in_specs=specs,
        out_specs=[o_spec, *u_out_specs],
        out_shape=[jax.ShapeDtypeStruct((m, n), out_dtype), *u_shapes],
        scratch_shapes=scratch,
        compiler_params=_params("arbitrary", "arbitrary"),
        name=name,
    )(*ins)
    return out if update is not None else out[0]


def _mm_acc_kernel(*refs, update):
    if update:
        a_ref, w_ref, src_ref, new_ref, o_ref, dst_ref, stage, sem = refs
        step = ((pl.program_id(0) * pl.num_programs(1) + pl.program_id(1)) * pl.num_programs(2)
                + pl.program_id(2))
        _buffer_update_step(step, update, src_ref, new_ref, dst_ref, stage, sem)
    else:
        a_ref, w_ref, o_ref = refs

    @pl.when(pl.program_id(2) == 0)
    def _():
        o_ref[...] = jnp.zeros_like(o_ref)

    o_ref[...] += _dot(a_ref[...], w_ref[...])


def _matmul_ksplit(a, w, *, tn, tk, update=None, name):
    m, k = a.shape
    n = w.shape[1]
    tm = _tile(m, 1024, SUBLANES)
    tn = _tile(n, tn, LANES)
    tk = _tile(k, tk, LANES)
    grid = (m // tm, n // tn, k // tk)
    u_in, u_specs, u_out_specs, u_shapes, scratch = _update_args(update, grid[0] * grid[1] * grid[2])
    out = pl.pallas_call(
        functools.partial(_mm_acc_kernel, update=update and grid[0] * grid[1] * grid[2]),
        grid=grid,
        in_specs=[pl.BlockSpec((tm, tk), lambda i, j, l: (i, l)),
                  pl.BlockSpec((tk, tn), lambda i, j, l: (l, j)), *u_specs],
        out_specs=[pl.BlockSpec((tm, tn), lambda i, j, l: (i, j)), *u_out_specs],
        out_shape=[jax.ShapeDtypeStruct((m, n), F32), *u_shapes],
        scratch_shapes=scratch,
        compiler_params=_params("arbitrary", "arbitrary", "arbitrary"),
        name=name,
    )(a, w, *u_in)
    return out if update is not None else out[0]


def _merge_kernel(att_ref, hg_ref, wa_ref, wh_ref, ga_ref, gh_ref, o_ref):
    pa = _dot(att_ref[...], wa_ref[...])
    ph = _dot(hg_ref[...], wh_ref[...])
    o_ref[...] = (ga_ref[...] * pa + gh_ref[...] * ph).astype(o_ref.dtype)


def _merge_proj(att, hg, w_att, w_hg, gates):
    m, ka = att.shape
    kh = hg.shape[1]
    d = w_att.shape[1]
    tm = _tile(m, 1024, SUBLANES)
    tn = _tile(d, 1024, LANES)
    nj = d // tn
    return pl.pallas_call(
        _merge_kernel,
        grid=(m // tm, nj),
        in_specs=[pl.BlockSpec((tm, ka), lambda i, j: (i, 0)),
                  pl.BlockSpec((tm, kh), lambda i, j: (i, 0)),
                  pl.BlockSpec((ka, tn), lambda i, j: (0, j)),
                  pl.BlockSpec((kh, tn), lambda i, j: (0, j)),
                  pl.BlockSpec((tm, tn), lambda i, j: (i, j)),
                  pl.BlockSpec((tm, tn), lambda i, j: (i, j + nj))],
        out_specs=pl.BlockSpec((tm, tn), lambda i, j: (i, j)),
        out_shape=jax.ShapeDtypeStruct((m, d), BF16),
        compiler_params=_params("parallel", "arbitrary"),
        name="merge_proj",
    )(att, hg, w_att, w_hg, gates, gates)


def _attn_prompt_kernel(q_ref, kp_ref, kc_ref, vp_ref, vc_ref, o_ref, lse_ref, *, dilation):
    jb = pl.program_id(1)
    blk = ATT_BLOCK
    r = lax.broadcasted_iota(jnp.int32, (blk, 2 * blk), 0)
    c = lax.broadcasted_iota(jnp.int32, (blk, 2 * blk), 1)
    dist = blk + r - c
    mask = (dist >= 0) & (dist <= blk) & ((c >= blk) | (jb > 0))
    for res in range(dilation):
        rows = pl.ds(res, blk, stride=dilation) if dilation > 1 else slice(None)
        for h in range(q_ref.shape[1] // ATT_HEAD_DIM):
            sl = slice(h * ATT_HEAD_DIM, (h + 1) * ATT_HEAD_DIM)
            q = q_ref[rows, sl].astype(BF16)
            k = jnp.concatenate([kp_ref[rows, sl], kc_ref[rows, sl]], axis=0).astype(BF16)
            v = jnp.concatenate([vp_ref[rows, sl], vc_ref[rows, sl]], axis=0).astype(BF16)
            s = jnp.where(mask, _dot_nt(q, k) * ATT_SCALE, NEG_BIG)
            m = jnp.max(s, axis=-1, keepdims=True)
            p = jnp.exp(s - m)
            l = jnp.sum(p, axis=-1, keepdims=True)
            o_ref[rows, sl] = _dot(p.astype(BF16), v) / l
            lse_ref[rows, sl] = jnp.broadcast_to(m + jnp.log(l), (blk, ATT_HEAD_DIM))


def _attn_prompt(proj, seq, group, dilation):
    chunk = ATT_BLOCK * dilation
    nch = seq // chunk
    lw = ATT_OUT_WIDTH if dilation == 1 else ATT_HEAD_DIM
    nhb = ATT_OUT_WIDTH // lw
    ngroups = len(ATT_GROUPS)

    def spec(col, prev):
        c0 = col * nhb
        if prev:
            return pl.BlockSpec((chunk, lw), lambda hb, j: (jnp.maximum(j - 1, 0), c0 + hb))
        return pl.BlockSpec((chunk, lw), lambda hb, j: (j, c0 + hb))

    out_spec = pl.BlockSpec((chunk, lw), lambda hb, j: (j, hb))
    shape = jax.ShapeDtypeStruct((seq, ATT_OUT_WIDTH), F32)
    return pl.pallas_call(
        functools.partial(_attn_prompt_kernel, dilation=dilation),
        grid=(nhb, nch),
        in_specs=[spec(group, False),
                  spec(ngroups + group, True), spec(ngroups + group, False),
                  spec(2 * ngroups + group, True), spec(2 * ngroups + group, False)],
        out_specs=[out_spec, out_spec],
        out_shape=[shape, shape],
        compiler_params=_params("parallel", "arbitrary"),
        name=f"attn_prompt_d{dilation}",
    )(proj, proj, proj, proj, proj)


def _attn_sample_kernel(*refs, dilation, t_new, update):
    if update:
        (q_ref, kn_ref, vn_ref, kb_ref, vb_ref, kn2_ref, vn2_ref, kx_ref, vx_ref,
         o_ref, lse_ref, ko_ref, vo_ref, m_sc, l_sc, acc_sc) = refs
    else:
        q_ref, kn_ref, vn_ref, kb_ref, vb_ref, o_ref, lse_ref, m_sc, l_sc, acc_sc = refs
    lt = pl.program_id(1)
    nlt = pl.num_programs(1)
    nh = ATT_SLOTS
    tl = kb_ref.shape[1] // nh
    shift = t_new * nh
    dmask = dilation - 1
    rows = nh * t_new

    def lanes(h):
        return slice(h * ATT_HEAD_DIM, (h + 1) * ATT_HEAD_DIM)

    def per_head(fn):
        return jnp.concatenate([fn(h) for h in range(nh)], axis=0)

    def head_rows(x, h):
        return x[h * t_new:(h + 1) * t_new]

    q16 = [q_ref[0, :, lanes(h)].astype(BF16) for h in range(nh)]

    @pl.when(lt == 0)
    def _():
        s = per_head(lambda h: _dot_nt(q16[h], kn_ref[0, :, lanes(h)].astype(BF16))) * ATT_SCALE
        t = lax.broadcasted_iota(jnp.int32, (rows, t_new), 0) % t_new
        i = lax.broadcasted_iota(jnp.int32, (rows, t_new), 1)
        s = jnp.where((i <= t) & (((t - i) & dmask) == 0), s, NEG_BIG)
        m = jnp.max(s, axis=-1, keepdims=True)
        p = jnp.exp(s - m)
        m_sc[...] = m
        l_sc[...] = jnp.sum(p, axis=-1, keepdims=True)
        acc_sc[...] = per_head(lambda h: _dot(head_rows(p, h).astype(BF16),
                                              vn_ref[0, :, lanes(h)].astype(BF16)))

    s = per_head(lambda h: _dot_nt(q16[h], kb_ref[0, pl.ds(h, tl, stride=nh), :].astype(BF16))) * ATT_SCALE
    t = lax.broadcasted_iota(jnp.int32, (rows, tl), 0) % t_new
    j = lax.broadcasted_iota(jnp.int32, (rows, tl), 1) + lt * tl
    s = jnp.where((j >= t) & (((j - t) & dmask) == 0), s, NEG_BIG)
    m_old = m_sc[...]
    m_new = jnp.maximum(m_old, jnp.max(s, axis=-1, keepdims=True))
    alpha = jnp.exp(m_old - m_new)
    p = jnp.exp(s - m_new)
    l_sc[...] = alpha * l_sc[...] + jnp.sum(p, axis=-1, keepdims=True)
    pv = per_head(lambda h: _dot(head_rows(p, h).astype(BF16),
                                 vb_ref[0, pl.ds(h, tl, stride=nh), :].astype(BF16)))
    acc_sc[...] = alpha * acc_sc[...] + pv
    m_sc[...] = m_new

    if update:
        keep = tl * nh - shift
        ko_ref[0, :keep] = kb_ref[0, shift:]
        vo_ref[0, :keep] = vb_ref[0, shift:]

        @pl.when(lt < nlt - 1)
        def _():
            ko_ref[0, keep:] = kx_ref[0]
            vo_ref[0, keep:] = vx_ref[0]

        @pl.when(lt == nlt - 1)
        def _():
            ko_ref[0, keep:] = kn2_ref[0]
            vo_ref[0, keep:] = vn2_ref[0]

    @pl.when(lt == nlt - 1)
    def _():
        l = l_sc[...]
        out = acc_sc[...] / l
        lse = m_sc[...] + jnp.log(l)
        for h in range(nh):
            o_ref[0, :, lanes(h)] = head_rows(out, h)
            lse_ref[0, :, lanes(h)] = jnp.broadcast_to(head_rows(lse, h), (t_new, ATT_HEAD_DIM))


def _new_rows(proj3, batch0, col):
    db = proj3.shape[0] - batch0
    c0 = col * ATT_OUT_WIDTH
    return proj3[batch0:, :, c0:c0 + ATT_OUT_WIDTH].reshape(db, proj3.shape[1] * ATT_SLOTS, ATT_HEAD_DIM)


def _attn_sample(proj3, batch0, group, dilation, k_buf, v_buf, *, update):
    db, rows_total, _ = k_buf.shape
    nh = ATT_SLOTS
    length = rows_total // nh
    t_new = proj3.shape[1]
    ngroups = len(ATT_GROUPS)
    tl = _tile(length, 1024 if update else 2048, SUBLANES)
    nlt = length // tl
    step = tl // t_new
    last = length // t_new - 1
    shift = t_new * nh

    def new_spec(col):
        return pl.BlockSpec((1, t_new, ATT_OUT_WIDTH), lambda b, l: (batch0 + b, 0, col))

    new2_spec = pl.BlockSpec((1, shift, ATT_HEAD_DIM), lambda b, l: (b, 0, 0))
    buf_spec = pl.BlockSpec((1, tl * nh, ATT_HEAD_DIM), lambda b, l: (b, l, 0))
    next_spec = pl.BlockSpec((1, shift, ATT_HEAD_DIM),
                             lambda b, l: (b, jnp.minimum((l + 1) * step, last), 0))
    small = pl.BlockSpec((1, t_new, ATT_OUT_WIDTH), lambda b, l: (b, 0, 0))
    small_shape = jax.ShapeDtypeStruct((db, t_new, ATT_OUT_WIDTH), F32)
    buf_shape = jax.ShapeDtypeStruct(k_buf.shape, F32)
    rows = nh * t_new
    ins = [proj3, proj3, proj3, k_buf, v_buf]
    in_specs = [new_spec(group), new_spec(ngroups + group), new_spec(2 * ngroups + group),
                buf_spec, buf_spec]
    out_specs, out_shape = [small, small], [small_shape, small_shape]
    if update:
        ins += [_new_rows(proj3, batch0, ngroups + group), _new_rows(proj3, batch0, 2 * ngroups + group),
                k_buf, v_buf]
        in_specs += [new2_spec, new2_spec, next_spec, next_spec]
        out_specs += [buf_spec, buf_spec]
        out_shape += [buf_shape, buf_shape]
    return pl.pallas_call(
        functools.partial(_attn_sample_kernel, dilation=dilation, t_new=t_new, update=update),
        grid=(db, nlt),
        in_specs=in_specs,
        out_specs=out_specs,
        out_shape=out_shape,
        scratch_shapes=[pltpu.VMEM((rows, 1), F32), pltpu.VMEM((rows, 1), F32),
                        pltpu.VMEM((rows, ATT_HEAD_DIM), F32)],
        compiler_params=_params("parallel", "arbitrary"),
        name=f"attn_sample_d{dilation}",
    )(*ins)


def _attn_sample_sparse_kernel(q_ref, kn_ref, vn_ref, kb_ref, vb_ref, o_ref, lse_ref, *, dilation, t_new):
    nh = ATT_SLOTS
    na = kb_ref.shape[1]
    rows = nh * t_new
    nkeys = na * t_new
    dmask = dilation - 1

    def lanes(h):
        return slice(h * ATT_HEAD_DIM, (h + 1) * ATT_HEAD_DIM)

    def per_head(fn):
        return jnp.concatenate([fn(h) for h in range(nh)], axis=0)

    def head_rows(x, h):
        return x[h * t_new:(h + 1) * t_new]

    def head_keys(ref, h):
        return ref[0, :, pl.ds(h, t_new, stride=nh), :].reshape(nkeys, ATT_HEAD_DIM).astype(BF16)

    q16 = [q_ref[0, :, lanes(h)].astype(BF16) for h in range(nh)]
    sn = per_head(lambda h: _dot_nt(q16[h], kn_ref[0, :, lanes(h)].astype(BF16))) * ATT_SCALE
    t = lax.broadcasted_iota(jnp.int32, (rows, t_new), 0) % t_new
    i = lax.broadcasted_iota(jnp.int32, (rows, t_new), 1)
    sn = jnp.where((i <= t) & (((t - i) & dmask) == 0), sn, NEG_BIG)
    sb = per_head(lambda h: _dot_nt(q16[h], head_keys(kb_ref, h))) * ATT_SCALE
    t = lax.broadcasted_iota(jnp.int32, (rows, nkeys), 0) % t_new
    c = lax.broadcasted_iota(jnp.int32, (rows, nkeys), 1) % t_new
    sb = jnp.where(c == t, sb, NEG_BIG)
    m = jnp.maximum(jnp.max(sn, axis=-1, keepdims=True), jnp.max(sb, axis=-1, keepdims=True))
    pn = jnp.exp(sn - m)
    pb = jnp.exp(sb - m)
    l = jnp.sum(pn, axis=-1, keepdims=True) + jnp.sum(pb, axis=-1, keepdims=True)
    acc = per_head(lambda h: _dot(head_rows(pn, h).astype(BF16), vn_ref[0, :, lanes(h)].astype(BF16))
                   + _dot(head_rows(pb, h).astype(BF16), head_keys(vb_ref, h)))
    out = acc / l
    lse = m + jnp.log(l)
    for h in range(nh):
        o_ref[0, :, lanes(h)] = head_rows(out, h)
        lse_ref[0, :, lanes(h)] = jnp.broadcast_to(head_rows(lse, h), (t_new, ATT_HEAD_DIM))


def _attn_sample_sparse(proj3, batch0, group, dilation, k_buf, v_buf):
    db, rows_total, _ = k_buf.shape
    nh = ATT_SLOTS
    t_new = proj3.shape[1]
    ngroups = len(ATT_GROUPS)
    na = rows_total // (dilation * nh)
    view = (db, na, dilation * nh, ATT_HEAD_DIM)

    def new_spec(col):
        return pl.BlockSpec((1, t_new, ATT_OUT_WIDTH), lambda b: (batch0 + b, 0, col))

    buf_spec = pl.BlockSpec((1, na, t_new * nh, ATT_HEAD_DIM), lambda b: (b, 0, 0, 0))
    small = pl.BlockSpec((1, t_new, ATT_OUT_WIDTH), lambda b: (b, 0, 0))
    small_shape = jax.ShapeDtypeStruct((db, t_new, ATT_OUT_WIDTH), F32)
    return pl.pallas_call(
        functools.partial(_attn_sample_sparse_kernel, dilation=dilation, t_new=t_new),
        grid=(db,),
        in_specs=[new_spec(group), new_spec(ngroups + group), new_spec(2 * ngroups + group),
                  buf_spec, buf_spec],
        out_specs=[small, small],
        out_shape=[small_shape, small_shape],
        compiler_params=_params("parallel"),
        name=f"attn_sample_d{dilation}",
    )(proj3, proj3, proj3, k_buf.reshape(view), v_buf.reshape(view))


def _combine_kernel(*refs, np_tiles):
    out_ref = refs[12]

    def mix(o0, o1, o2, l0, l1, l2):
        a, b, c = l0[...], l1[...], l2[...]
        m = jnp.maximum(jnp.maximum(a, b), c)
        ea, eb, ec = jnp.exp(a - m), jnp.exp(b - m), jnp.exp(c - m)
        num = ea * o0[...] + eb * o1[...] + ec * o2[...]
        out_ref[...] = (num / (ea + eb + ec)).astype(out_ref.dtype)

    i = pl.program_id(0)
    pl.when(i < np_tiles)(lambda: mix(*refs[0:6]))
    pl.when(i >= np_tiles)(lambda: mix(*refs[6:12]))


def _combine(outs_p, lses_p, outs_s, lses_s):
    n_prompt, w = outs_p[0].shape
    n_sample = outs_s[0].shape[0]
    tm, np_tiles, prompt, sample, stacked, _ = _split_rows(n_prompt, n_sample, w, 256)
    m = n_prompt + n_sample
    return pl.pallas_call(
        functools.partial(_combine_kernel, np_tiles=np_tiles),
        grid=(m // tm,),
        in_specs=[prompt] * 6 + [sample] * 6,
        out_specs=stacked,
        out_shape=jax.ShapeDtypeStruct((m, w), BF16),
        compiler_params=_params("arbitrary"),
        name="attn_combine",
    )(*outs_p, *lses_p, *outs_s, *lses_s)


def _cumsum_rows(x):
    c = x.shape[0]
    r = lax.broadcasted_iota(jnp.int32, (c, c), 0)
    s = lax.broadcasted_iota(jnp.int32, (c, c), 1)
    tri = jnp.where(r >= s, 1.0, 0.0).astype(BF16)
    hi = x.astype(BF16)
    rem = x - hi.astype(F32)
    mid = rem.astype(BF16)
    lo = (rem - mid.astype(F32)).astype(BF16)
    return _dot(tri, hi) + _dot(tri, mid) + _dot(tri, lo)


def _col_bcast(w):
    hi = w.astype(BF16).astype(F32)
    mid = (w - hi).astype(BF16).astype(F32)
    lo = (w - hi - mid).astype(BF16).astype(F32)
    terms = jnp.concatenate([hi, mid, lo, jnp.zeros((SUBLANES - 3, LANES), F32)], axis=0).astype(BF16)
    ones = jnp.where(lax.broadcasted_iota(jnp.int32, (SUBLANES, LANES), 0) < 3, 1.0, 0.0).astype(BF16)
    return _dot_tn(terms, ones)


def _lower_bound(lb_ref):
    a = lb_ref[...]
    e = jnp.exp(a - jnp.max(a, axis=0, keepdims=True))
    return e[0:1] / jnp.sum(e, axis=0, keepdims=True)


def _hgrn_chunk(qg, fg, ig, gg, lb, gain, st):
    c = qg.shape[0]
    q = qg * _sigmoid(qg) * HG_SCALE
    forget = lb + (1.0 - lb) * _sigmoid(fg)
    k = 1.0 - forget
    v = ig
    b = _cumsum_rows(jnp.log(forget))
    v16 = v.astype(BF16)

    nsub = c // HG_SUB
    b3 = b.reshape(nsub, HG_SUB, LANES)
    q3 = q.reshape(nsub, HG_SUB, LANES)
    k3 = k.reshape(nsub, HG_SUB, LANES)
    pos = lax.broadcasted_iota(jnp.int32, (nsub, HG_SUB, LANES), 1)
    row = lax.broadcasted_iota(jnp.int32, (c, c), 0)
    col = lax.broadcasted_iota(jnp.int32, (c, c), 1)
    scores = jnp.zeros((c, c), F32)
    for s in range(HG_SUB):
        diff = jnp.where(pos >= s, b3 - b3[:, s:s + 1, :], NEG_BIG)
        w = jnp.sum(q3 * k3[:, s:s + 1, :] * jnp.exp(diff), axis=-1, keepdims=True)
        w = jnp.broadcast_to(w.reshape(c, 1), (c, c))
        scores = jnp.where(col == (row // HG_SUB) * HG_SUB + s, w, scores)

    width = HG_SUB
    while width < c:
        pair = 2 * width
        bm = jnp.concatenate(
            [jnp.broadcast_to(b[p * pair + width - 1:p * pair + width], (pair, LANES))
             for p in range(c // pair)], axis=0)
        e = jnp.exp(-jnp.abs(b - bm))
        right = (lax.broadcasted_iota(jnp.int32, (c, LANES), 0) // width) % 2 == 1
        ql = jnp.where(right, q * e, 0.0).astype(BF16)
        kl = jnp.where(right, 0.0, k * e).astype(BF16)
        a = _dot_nt(ql, kl)
        scores = jnp.where((row // pair == col // pair) & (row // width != col // width) & (row > col),
                           a, scores)
        width = pair

    o = _dot(scores.astype(BF16), v16) + _dot_nt((q * jnp.exp(b)).astype(BF16), st.astype(BF16))
    b_last = b[c - 1:c]
    st_new = st * jnp.exp(b_last) + _dot_tn(v16, (k * jnp.exp(b_last - b)).astype(BF16))
    out = _rms(o, gain) * (gg * _sigmoid(gg))
    return out, st_new


def _hgrn_prompt_kernel(q_ref, f_ref, i_ref, g_ref, lb_ref, gain_ref, o_ref, s_ref, st_sc):
    tb = pl.program_id(1)

    @pl.when(tb == 0)
    def _():
        st_sc[...] = jnp.zeros_like(st_sc)

    nheads = st_sc.shape[0]
    gain = gain_ref[...]
    lbs = [_lower_bound(lb_ref.at[:, h * LANES:(h + 1) * LANES]) for h in range(nheads)]
    sts = [st_sc[h] for h in range(nheads)]
    for ci in range(q_ref.shape[0] // HG_CHUNK):
        rs = slice(ci * HG_CHUNK, (ci + 1) * HG_CHUNK)
        for h in range(nheads):
            sl = slice(h * LANES, (h + 1) * LANES)
            out, sts[h] = _hgrn_chunk(q_ref[rs, sl], f_ref[rs, sl], i_ref[rs, sl], g_ref[rs, sl],
                                      lbs[h], gain, sts[h])
            o_ref[rs, sl] = out.astype(o_ref.dtype)
    for h in range(nheads):
        st_sc[h] = sts[h]

    @pl.when(tb == pl.num_programs(1) - 1)
    def _():
        for h in range(nheads):
            s_ref[h] = sts[h].T


def _hgrn_prompt(proj, seq, hg_lower_bound, hg_norm, hg_width):
    heads = hg_width // HG_EXPAND
    hw = HG_EXPAND
    assert 3 * ATT_WIDTH % hw == 0
    base = 3 * ATT_WIDTH // hw
    per = hg_width // hw
    hpb = hw // HG_EXPAND
    tb = _tile(seq, 512, HG_CHUNK)

    def col(which):
        return pl.BlockSpec((tb, hw), lambda g, t: (t, base + which * per + g))

    depth1 = hg_lower_bound.shape[0]
    return pl.pallas_call(
        _hgrn_prompt_kernel,
        grid=(per, seq // tb),
        in_specs=[col(0), col(1), col(2), col(3),
                  pl.BlockSpec((depth1, hw), lambda g, t: (0, g)),
                  pl.BlockSpec((1, LANES), lambda g, t: (0, 0))],
        out_specs=[pl.BlockSpec((tb, hw), lambda g, t: (t, g)),
                   pl.BlockSpec((hpb, HG_EXPAND, HG_HEAD_V), lambda g, t: (g, 0, 0))],
        out_shape=[jax.ShapeDtypeStruct((seq, hg_width), BF16),
                   jax.ShapeDtypeStruct((heads, HG_EXPAND, HG_HEAD_V), F32)],
        scratch_shapes=[pltpu.VMEM((hpb, HG_EXPAND, HG_HEAD_V), F32)],
        compiler_params=_params("parallel", "arbitrary"),
        name="hgrn_prompt",
    )(proj, proj, proj, proj, hg_lower_bound, hg_norm.reshape(1, LANES))


def _hgrn_sample_kernel(q_ref, f_ref, i_ref, g_ref, lb_ref, gain_ref, s_ref, o_ref, so_ref):
    nheads = s_ref.shape[1]
    c = q_ref.shape[1]
    qg, fg, v, gg = q_ref[0], f_ref[0], i_ref[0], g_ref[0]
    lb = _lower_bound(lb_ref)
    gain = gain_ref[...]
    q = qg * _sigmoid(qg) * HG_SCALE
    forget = lb + (1.0 - lb) * _sigmoid(fg)
    k = 1.0 - forget
    b = _cumsum_rows(jnp.log(forget))
    b_last = b[c - 1:c]
    qdec = (q * jnp.exp(b)).astype(BF16)
    kdec = (k * jnp.exp(b_last - b)).astype(BF16)
    decay = jnp.exp(b_last)
    v16 = v.astype(BF16)
    t = lax.broadcasted_iota(jnp.int32, b.shape, 0)
    pair = [q * k[s:s + 1] * jnp.exp(jnp.where(t >= s, b - b[s:s + 1], NEG_BIG)) for s in range(c)]
    outs = []
    for h in range(nheads):
        sl = slice(h * LANES, (h + 1) * LANES)
        st = s_ref[0, h]
        o = _dot(qdec[:, sl], st.astype(BF16))
        for s in range(c):
            o = o + jnp.sum(pair[s][:, sl], axis=-1, keepdims=True) * v[s:s + 1, sl]
        so_ref[0, h] = st * _col_bcast(decay[:, sl]) + _dot_tn(kdec[:, sl], v16[:, sl])
        outs.append(_rms(o, gain))
    o_ref[0] = (jnp.concatenate(outs, axis=1) * (gg * _sigmoid(gg))).astype(o_ref.dtype)


def _hgrn_sample(proj3, batch0, state, hg_lower_bound, hg_norm):
    db, heads = state.shape[:2]
    t_new = proj3.shape[1]
    hg_width = heads * HG_EXPAND
    cw = _tile(hg_width, ATT_OUT_WIDTH, LANES)
    assert 3 * ATT_WIDTH % cw == 0
    base = 3 * ATT_WIDTH // cw
    per = hg_width // cw
    hb = cw // HG_EXPAND

    def col(which):
        return pl.BlockSpec((1, t_new, cw), lambda b, c: (batch0 + b, 0, base + which * per + c))

    depth1 = hg_lower_bound.shape[0]
    st_spec = pl.BlockSpec((1, hb, HG_EXPAND, HG_HEAD_V), lambda b, c: (b, c, 0, 0))
    return pl.pallas_call(
        _hgrn_sample_kernel,
        grid=(db, per),
        in_specs=[col(0), col(1), col(2), col(3),
                  pl.BlockSpec((depth1, cw), lambda b, c: (0, c)),
                  pl.BlockSpec((1, LANES), lambda b, c: (0, 0)),
                  st_spec],
        out_specs=[pl.BlockSpec((1, t_new, cw), lambda b, c: (b, 0, c)), st_spec],
        out_shape=[jax.ShapeDtypeStruct((db, t_new, hg_width), BF16),
                   jax.ShapeDtypeStruct(state.shape, F32)],
        compiler_params=_params("parallel", "parallel"),
        name="hgrn_sample",
    )(proj3, proj3, proj3, proj3, hg_lower_bound, hg_norm.reshape(1, LANES), state)


def kernel(x_prompt, x_sample, cache_k_w128, cache_v_w128, cache_k_w512, cache_v_w512,
           cache_k_w2048, cache_v_w2048, state_hgrn, hg_lower_bound, w_in, w_gate, b_gate,
           w_proj_att, w_proj_hg, w_out, hg_norm, norm_mix_pre, norm_mix_post,
           norm_ffn_pre, norm_ffn_post, w_up, w_down):
    assert w_in.shape[0] == 1, "single-layer trunk"
    batch, seq, d_model = x_prompt.shape
    db, t_new, _ = x_sample.shape
    assert batch == 1
    hg_width = w_proj_hg.shape[1]
    n_prompt = batch * seq
    n_sample = db * t_new
    caches = (cache_k_w128, cache_v_w128, cache_k_w512, cache_v_w512, cache_k_w2048, cache_v_w2048)

    xp = x_prompt.reshape(n_prompt, d_model)
    xs = x_sample.reshape(n_sample, d_model)
    h = _norm_cast(xp, xs, norm_mix_pre[0])
    proj = _matmul(h, w_in[0].astype(BF16), name="in_proj")
    gates = _matmul(h, w_gate[0].astype(BF16), bias=b_gate[0], name="gate_proj")
    in_width = proj.shape[1]
    proj3 = proj.reshape((n_prompt + n_sample) // t_new, t_new, in_width)
    batch0 = n_prompt // t_new

    ngroups = len(ATT_GROUPS)
    mlp_steps = ((n_prompt + n_sample) // _tile(n_prompt + n_sample, 1024, SUBLANES)
                 * (w_up.shape[2] // _tile(w_up.shape[2], 1024, LANES)))
    pending = [None, None]
    outs_p, lses_p, outs_s, lses_s, new_kv_p, new_kv_s = [], [], [], [], [], []
    for g, (window, dilation) in enumerate(ATT_GROUPS):
        o, lse = _attn_prompt(proj, seq, g, dilation)
        outs_p.append(o)
        lses_p.append(lse)
        keep = min(window, seq)
        for part in (1, 2):
            c0 = part * ATT_WIDTH + g * ATT_OUT_WIDTH
            rows = proj[n_prompt - keep:n_prompt, c0:c0 + ATT_OUT_WIDTH]
            new_kv_p.append(rows.reshape(1, batch, keep, ATT_SLOTS, ATT_HEAD_DIM))
        k_buf, v_buf = caches[2 * g][0], caches[2 * g + 1][0]
        length = k_buf.shape[1]
        assert length == window and length == dilation * ATT_BLOCK
        k_buf = k_buf.reshape(db, length * ATT_SLOTS, ATT_HEAD_DIM)
        v_buf = v_buf.reshape(db, length * ATT_SLOTS, ATT_HEAD_DIM)
        deferred = db + 1 <= mlp_steps and g == ngroups - 1
        if deferred and dilation >= t_new:
            res = _attn_sample_sparse(proj3, batch0, g, dilation, k_buf, v_buf)
        else:
            res = _attn_sample(proj3, batch0, g, dilation, k_buf, v_buf, update=not deferred)
        outs_s.append(res[0].reshape(n_sample, ATT_OUT_WIDTH))
        lses_s.append(res[1].reshape(n_sample, ATT_OUT_WIDTH))
        if deferred:
            pending = [(k_buf, _new_rows(proj3, batch0, ngroups + g)),
                       (v_buf, _new_rows(proj3, batch0, 2 * ngroups + g))]
        else:
            new_kv_s += list(res[2:])
    att = _combine(outs_p, lses_p, outs_s, lses_s)

    hg_p, state_p = _hgrn_prompt(proj, seq, hg_lower_bound, hg_norm[0], hg_width)
    hg_s, state_s = _hgrn_sample(proj3, batch0, state_hgrn[0], hg_lower_bound, hg_norm[0])
    hg = jnp.concatenate([hg_p, hg_s.reshape(n_sample, hg_width)], axis=0)

    merged = _merge_proj(att, hg, w_proj_att[0].astype(BF16), w_proj_hg[0].astype(BF16), gates)
    mixed = _matmul(merged, w_out[0].astype(BF16), name="out_proj")
    x1, h2 = _resid_norm(xp, xs, mixed, norm_mix_post[0], norm_ffn_pre[0])

    u = _matmul(h2, w_up[0].astype(BF16), out_dtype=BF16, act="relu2", update=pending[0], name="ffn_up")
    if pending[0] is not None:
        u, k_last = u
    z = _matmul_ksplit(u, w_down[0].astype(BF16), tn=1024, tk=2048, update=pending[1], name="ffn_down")
    if pending[1] is not None:
        z, v_last = z
        new_kv_s += [k_last, v_last]
    y_prompt, y_sample = _resid_final(x1, z, norm_ffn_post[0], n_prompt)
    y_prompt = y_prompt.reshape(batch, seq, d_model)
    y_sample = y_sample.reshape(db, t_new, d_model)
    heads = hg_width // HG_EXPAND
    new_kv_s = [c.reshape(src.shape) for c, src in zip(new_kv_s, caches)]
    return (y_prompt, y_sample, *new_kv_p,
            state_p.reshape(1, batch, heads, HG_EXPAND, HG_HEAD_V),
            *new_kv_s, state_s[None])
```

```python
import functools
import math

import jax
import jax.numpy as jnp
from jax import lax
from jax.experimental import pallas as pl
from jax.experimental.pallas import tpu as pltpu

ATT_HEAD_DIM = 128
ATT_SLOTS = 8
ATT_GROUPS = ((128, 1), (512, 4), (2048, 16))
ATT_BLOCK = 128
ATT_OUT_WIDTH = ATT_SLOTS * ATT_HEAD_DIM
ATT_WIDTH = len(ATT_GROUPS) * ATT_OUT_WIDTH
ATT_SCALE = ATT_HEAD_DIM ** -0.5
HG_EXPAND = 128
HG_HEAD_V = 128
HG_SCALE = HG_EXPAND ** -0.5
HG_CHUNK = 64
HG_SUB = 8
RMS_EPS = 1e-6
NEG_BIG = -1e30

V7X_VMEM_BYTES = 64 * 1024 * 1024
VMEM_LIMIT = V7X_VMEM_BYTES - 2 * 1024 * 1024
LANES = 128
SUBLANES = 8

BF16 = jnp.bfloat16
F32 = jnp.float32


def _params(*sem):
    return pltpu.CompilerParams(dimension_semantics=sem, vmem_limit_bytes=VMEM_LIMIT)


def _tile(n, target, mult):
    if n <= target:
        return n
    t = (target // mult) * mult
    while t >= mult:
        if n % t == 0:
            return t
        t -= mult
    raise ValueError(f"no tile for {n}")


def _sigmoid(x):
    return 1.0 / (1.0 + jnp.exp(-x))


def _dot(a, b):
    return jnp.dot(a, b, preferred_element_type=F32)


def _dot_nt(a, b):
    return lax.dot_general(a, b, (((1,), (1,)), ((), ())), preferred_element_type=F32)


def _dot_tn(a, b):
    return lax.dot_general(a, b, (((0,), (0,)), ((), ())), preferred_element_type=F32)


def _rms(x, gain):
    return x * lax.rsqrt(jnp.mean(x * x, axis=-1, keepdims=True) + RMS_EPS) * gain


def _split_rows(n_prompt, n_sample, d, target):
    tm = _tile(math.gcd(n_prompt, n_sample), target, SUBLANES)
    np_tiles = n_prompt // tm
    prompt = pl.BlockSpec((tm, d), lambda i: (jnp.minimum(i, np_tiles - 1), 0))
    sample = pl.BlockSpec((tm, d), lambda i: (jnp.maximum(i - np_tiles, 0), 0))
    stacked = pl.BlockSpec((tm, d), lambda i: (i, 0))
    vec = pl.BlockSpec((1, d), lambda i: (0, 0))
    return tm, np_tiles, prompt, sample, stacked, vec


def _norm_cast_kernel(xp_ref, xs_ref, g_ref, o_ref, *, np_tiles):
    i = pl.program_id(0)

    @pl.when(i < np_tiles)
    def _():
        o_ref[...] = _rms(xp_ref[...], g_ref[...]).astype(o_ref.dtype)

    @pl.when(i >= np_tiles)
    def _():
        o_ref[...] = _rms(xs_ref[...], g_ref[...]).astype(o_ref.dtype)


def _norm_cast(xp, xs, gain):
    d = xp.shape[1]
    m = xp.shape[0] + xs.shape[0]
    tm, np_tiles, prompt, sample, stacked, vec = _split_rows(xp.shape[0], xs.shape[0], d, 256)
    return pl.pallas_call(
        functools.partial(_norm_cast_kernel, np_tiles=np_tiles),
        grid=(m // tm,),
        in_specs=[prompt, sample, vec],
        out_specs=stacked,
        out_shape=jax.ShapeDtypeStruct((m, d), BF16),
        compiler_params=_params("arbitrary"),
        name="norm_cast",
    )(xp, xs, gain.reshape(1, d))


def _resid_norm_kernel(xp_ref, xs_ref, y_ref, gpost_ref, gpre_ref, x1_ref, h_ref, *, np_tiles):
    i = pl.program_id(0)

    def body(x_ref):
        x1 = x_ref[...] + _rms(y_ref[...], gpost_ref[...])
        x1_ref[...] = x1
        h_ref[...] = _rms(x1, gpre_ref[...]).astype(h_ref.dtype)

    pl.when(i < np_tiles)(lambda: body(xp_ref))
    pl.when(i >= np_tiles)(lambda: body(xs_ref))


def _resid_norm(xp, xs, y, gain_post, gain_pre):
    m, d = y.shape
    tm, np_tiles, prompt, sample, stacked, vec = _split_rows(xp.shape[0], xs.shape[0], d, 256)
    return pl.pallas_call(
        functools.partial(_resid_norm_kernel, np_tiles=np_tiles),
        grid=(m // tm,),
        in_specs=[prompt, sample, stacked, vec, vec],
        out_specs=[stacked, stacked],
        out_shape=[jax.ShapeDtypeStruct((m, d), F32), jax.ShapeDtypeStruct((m, d), BF16)],
        compiler_params=_params("arbitrary"),
        name="resid_norm",
    )(xp, xs, y, gain_post.reshape(1, d), gain_pre.reshape(1, d))


def _resid_final_kernel(x_ref, y_ref, g_ref, op_ref, os_ref, *, np_tiles):
    i = pl.program_id(0)

    @pl.when(i < np_tiles)
    def _():
        op_ref[...] = x_ref[...] + _rms(y_ref[...], g_ref[...])

    @pl.when(i >= np_tiles)
    def _():
        os_ref[...] = x_ref[...] + _rms(y_ref[...], g_ref[...])


def _resid_final(x, y, gain, n_prompt):
    m, d = x.shape
    tm, np_tiles, prompt, sample, stacked, vec = _split_rows(n_prompt, m - n_prompt, d, 256)
    return pl.pallas_call(
        functools.partial(_resid_final_kernel, np_tiles=np_tiles),
        grid=(m // tm,),
        in_specs=[stacked, stacked, vec],
        out_specs=[prompt, sample],
        out_shape=[jax.ShapeDtypeStruct((n_prompt, d), F32),
                   jax.ShapeDtypeStruct((m - n_prompt, d), F32)],
        compiler_params=_params("arbitrary"),
        name="resid_final",
    )(x, y, gain.reshape(1, d))


def _buffer_update_step(step, nsteps, src_ref, new_ref, dst_ref, stage, sem):
    nseq, rows, _ = src_ref.shape
    shift = new_ref.shape[1]
    body = rows - shift
    every = nsteps // (nseq + 1)
    turn = step // every
    active = step % every == 0

    def load(b):
        return pltpu.make_async_copy(src_ref.at[b, pl.ds(shift, body)], stage.at[b % 2], sem.at[0, b % 2])

    def store(b):
        return pltpu.make_async_copy(stage.at[b % 2], dst_ref.at[b, pl.ds(0, body)], sem.at[1, b % 2])

    def tail(b):
        return pltpu.make_async_copy(new_ref.at[b], dst_ref.at[b, pl.ds(body, shift)], sem.at[2, b % 2])

    @pl.when(active & (turn == 0))
    def _():
        load(0).start(priority=1)

    @pl.when(active & (turn < nseq))
    def _():
        load(turn).wait()
        store(turn).start(priority=1)
        tail(turn).start(priority=1)

    @pl.when(active & (turn >= 1) & (turn <= nseq))
    def _():
        store(turn - 1).wait()
        tail(turn - 1).wait()

    @pl.when(active & (turn + 1 < nseq))
    def _():
        load(turn + 1).start(priority=1)


def _mm_kernel(*refs, act, update):
    if update:
        a_ref, w_ref, src_ref, new_ref, o_ref, dst_ref, stage, sem = refs
        step = pl.program_id(0) * pl.num_programs(1) + pl.program_id(1)
        _buffer_update_step(step, update, src_ref, new_ref, dst_ref, stage, sem)
    else:
        a_ref, w_ref, o_ref = refs
    acc = _dot(a_ref[...], w_ref[...])
    if act == "relu2":
        acc = jnp.maximum(acc, 0.0)
        acc = acc * acc
    o_ref[...] = acc.astype(o_ref.dtype)


def _mm_bias_sigmoid_kernel(a_ref, w_ref, b_ref, o_ref):
    o_ref[...] = _sigmoid(_dot(a_ref[...], w_ref[...]) + b_ref[...]).astype(o_ref.dtype)


def _update_args(update, nsteps):
    if update is None:
        return (), [], [], [], []
    src, new = update
    assert nsteps >= src.shape[0] + 1, "grid too short to finish the buffer update"
    any_spec = pl.BlockSpec(memory_space=pl.ANY)
    stage = pltpu.VMEM((2, src.shape[1] - new.shape[1], src.shape[2]), src.dtype)
    return ((src, new), [any_spec, any_spec], [any_spec],
            [jax.ShapeDtypeStruct(src.shape, src.dtype)], [stage, pltpu.SemaphoreType.DMA((3, 2))])


def _matmul(a, w, *, out_dtype=F32, act=None, bias=None, update=None, name="matmul"):
    m, k = a.shape
    n = w.shape[1]
    tm = _tile(m, 1024, SUBLANES)
    tn = _tile(n, 1024, LANES)
    grid = (m // tm, n // tn)
    a_spec = pl.BlockSpec((tm, k), lambda i, j: (i, 0))
    w_spec = pl.BlockSpec((k, tn), lambda i, j: (0, j))
    o_spec = pl.BlockSpec((tm, tn), lambda i, j: (i, j))
    u_in, u_specs, u_out_specs, u_shapes, scratch = _update_args(update, grid[0] * grid[1])
    if bias is None:
        body = functools.partial(_mm_kernel, act=act, update=update and grid[0] * grid[1])
        ins, specs = (a, w, *u_in), [a_spec, w_spec, *u_specs]
    else:
        assert update is None
        body, ins = _mm_bias_sigmoid_kernel, (a, w, bias.reshape(1, n))
        specs = [a_spec, w_spec, pl.BlockSpec((1, tn), lambda i, j: (0, j))]
    out = pl.pallas_call(
        body,
        grid=grid,
        in_specs=specs,
        out_specs=[o_spec, *u_out_specs],
        out_shape=[jax.ShapeDtypeStruct((m, n), out_dtype), *u_shapes],
        scratch_shapes=scratch,
        compiler_params=_params("arbitrary", "arbitrary"),
        name=name,
    )(*ins)
    return out if update is not None else out[0]


def _mm_acc_kernel(*refs, update):
    if update:
        a_ref, w_ref, src_ref, new_ref, o_ref, dst_ref, stage, sem = refs
        step = ((pl.program_id(0) * pl.num_programs(1) + pl.program_id(1)) * pl.num_programs(2)
                + pl.program_id(2))
        _buffer_update_step(step, update, src_ref, new_ref, dst_ref, stage, sem)
    else:
        a_ref, w_ref, o_ref = refs

    @pl.when(pl.program_id(2) == 0)
    def _():
        o_ref[...] = jnp.zeros_like(o_ref)

    o_ref[...] += _dot(a_ref[...], w_ref[...])


def _matmul_ksplit(a, w, *, tn, tk, update=None, name):
    m, k = a.shape
    n = w.shape[1]
    tm = _tile(m, 1024, SUBLANES)
    tn = _tile(n, tn, LANES)
    tk = _tile(k, tk, LANES)
    grid = (m // tm, n // tn, k // tk)
    u_in, u_specs, u_out_specs, u_shapes, scratch = _update_args(update, grid[0] * grid[1] * grid[2])
    out = pl.pallas_call(
        functools.partial(_mm_acc_kernel, update=update and grid[0] * grid[1] * grid[2]),
        grid=grid,
        in_specs=[pl.BlockSpec((tm, tk), lambda i, j, l: (i, l)),
                  pl.BlockSpec((tk, tn), lambda i, j, l: (l, j)), *u_specs],
        out_specs=[pl.BlockSpec((tm, tn), lambda i, j, l: (i, j)), *u_out_specs],
        out_shape=[jax.ShapeDtypeStruct((m, n), F32), *u_shapes],
        scratch_shapes=scratch,
        compiler_params=_params("arbitrary", "arbitrary", "arbitrary"),
        name=name,
    )(a, w, *u_in)
    return out if update is not None else out[0]


def _merge_kernel(att_ref, hg_ref, wa_ref, wh_ref, ga_ref, gh_ref, o_ref):
    pa = _dot(att_ref[...], wa_ref[...])
    ph = _dot(hg_ref[...], wh_ref[...])
    o_ref[...] = (ga_ref[...] * pa + gh_ref[...] * ph).astype(o_ref.dtype)


def _merge_proj(att, hg, w_att, w_hg, gates):
    m, ka = att.shape
    kh = hg.shape[1]
    d = w_att.shape[1]
    tm = _tile(m, 1024, SUBLANES)
    tn = _tile(d, 1024, LANES)
    nj = d // tn
    return pl.pallas_call(
        _merge_kernel,
        grid=(m // tm, nj),
        in_specs=[pl.BlockSpec((tm, ka), lambda i, j: (i, 0)),
                  pl.BlockSpec((tm, kh), lambda i, j: (i, 0)),
                  pl.BlockSpec((ka, tn), lambda i, j: (0, j)),
                  pl.BlockSpec((kh, tn), lambda i, j: (0, j)),
                  pl.BlockSpec((tm, tn), lambda i, j: (i, j)),
                  pl.BlockSpec((tm, tn), lambda i, j: (i, j + nj))],
        out_specs=pl.BlockSpec((tm, tn), lambda i, j: (i, j)),
        out_shape=jax.ShapeDtypeStruct((m, d), BF16),
        compiler_params=_params("parallel", "arbitrary"),
        name="merge_proj",
    )(att, hg, w_att, w_hg, gates, gates)


def _attn_prompt_kernel(q_ref, kp_ref, kc_ref, vp_ref, vc_ref, o_ref, lse_ref, *, dilation):
    jb = pl.program_id(1)
    blk = ATT_BLOCK
    r = lax.broadcasted_iota(jnp.int32, (blk, 2 * blk), 0)
    c = lax.broadcasted_iota(jnp.int32, (blk, 2 * blk), 1)
    dist = blk + r - c
    mask = (dist >= 0) & (dist <= blk) & ((c >= blk) | (jb > 0))
    for res in range(dilation):
        rows = pl.ds(res, blk, stride=dilation) if dilation > 1 else slice(None)
        for h in range(q_ref.shape[1] // ATT_HEAD_DIM):
            sl = slice(h * ATT_HEAD_DIM, (h + 1) * ATT_HEAD_DIM)
            q = q_ref[rows, sl].astype(BF16)
            k = jnp.concatenate([kp_ref[rows, sl], kc_ref[rows, sl]], axis=0).astype(BF16)
            v = jnp.concatenate([vp_ref[rows, sl], vc_ref[rows, sl]], axis=0).astype(BF16)
            s = jnp.where(mask, _dot_nt(q, k) * ATT_SCALE, NEG_BIG)
            m = jnp.max(s, axis=-1, keepdims=True)
            p = jnp.exp(s - m)
            l = jnp.sum(p, axis=-1, keepdims=True)
            o_ref[rows, sl] = _dot(p.astype(BF16), v) / l
            lse_ref[rows, sl] = jnp.broadcast_to(m + jnp.log(l), (blk, ATT_HEAD_DIM))


def _attn_prompt(proj, seq, group, dilation):
    chunk = ATT_BLOCK * dilation
    nch = seq // chunk
    lw = ATT_OUT_WIDTH if dilation == 1 else ATT_HEAD_DIM
    nhb = ATT_OUT_WIDTH // lw
    ngroups = len(ATT_GROUPS)

    def spec(col, prev):
        c0 = col * nhb
        if prev:
            return pl.BlockSpec((chunk, lw), lambda hb, j: (jnp.maximum(j - 1, 0), c0 + hb))
        return pl.BlockSpec((chunk, lw), lambda hb, j: (j, c0 + hb))

    out_spec = pl.BlockSpec((chunk, lw), lambda hb, j: (j, hb))
    shape = jax.ShapeDtypeStruct((seq, ATT_OUT_WIDTH), F32)
    return pl.pallas_call(
        functools.partial(_attn_prompt_kernel, dilation=dilation),
        grid=(nhb, nch),
        in_specs=[spec(group, False),
                  spec(ngroups + group, True), spec(ngroups + group, False),
                  spec(2 * ngroups + group, True), spec(2 * ngroups + group, False)],
        out_specs=[out_spec, out_spec],
        out_shape=[shape, shape],
        compiler_params=_params("parallel", "arbitrary"),
        name=f"attn_prompt_d{dilation}",
    )(proj, proj, proj, proj, proj)


def _attn_sample_kernel(*refs, dilation, t_new, update):
    if update:
        (q_ref, kn_ref, vn_ref, kb_ref, vb_ref, kn2_ref, vn2_ref, kx_ref, vx_ref,
         o_ref, lse_ref, ko_ref, vo_ref, m_sc, l_sc, acc_sc) = refs
    else:
        q_ref, kn_ref, vn_ref, kb_ref, vb_ref, o_ref, lse_ref, m_sc, l_sc, acc_sc = refs
    lt = pl.program_id(1)
    nlt = pl.num_programs(1)
    nh = ATT_SLOTS
    tl = kb_ref.shape[1] // nh
    shift = t_new * nh
    dmask = dilation - 1
    rows = nh * t_new

    def lanes(h):
        return slice(h * ATT_HEAD_DIM, (h + 1) * ATT_HEAD_DIM)

    def per_head(fn):
        return jnp.concatenate([fn(h) for h in range(nh)], axis=0)

    def head_rows(x, h):
        return x[h * t_new:(h + 1) * t_new]

    q16 = [q_ref[0, :, lanes(h)].astype(BF16) for h in range(nh)]

    @pl.when(lt == 0)
    def _():
        s = per_head(lambda h: _dot_nt(q16[h], kn_ref[0, :, lanes(h)].astype(BF16))) * ATT_SCALE
        t = lax.broadcasted_iota(jnp.int32, (rows, t_new), 0) % t_new
        i = lax.broadcasted_iota(jnp.int32, (rows, t_new), 1)
        s = jnp.where((i <= t) & (((t - i) & dmask) == 0), s, NEG_BIG)
        m = jnp.max(s, axis=-1, keepdims=True)
        p = jnp.exp(s - m)
        m_sc[...] = m
        l_sc[...] = jnp.sum(p, axis=-1, keepdims=True)
        acc_sc[...] = per_head(lambda h: _dot(head_rows(p, h).astype(BF16),
                                              vn_ref[0, :, lanes(h)].astype(BF16)))

    s = per_head(lambda h: _dot_nt(q16[h], kb_ref[0, pl.ds(h, tl, stride=nh), :].astype(BF16))) * ATT_SCALE
    t = lax.broadcasted_iota(jnp.int32, (rows, tl), 0) % t_new
    j = lax.broadcasted_iota(jnp.int32, (rows, tl), 1) + lt * tl
    s = jnp.where((j >= t) & (((j - t) & dmask) == 0), s, NEG_BIG)
    m_old = m_sc[...]
    m_new = jnp.maximum(m_old, jnp.max(s, axis=-1, keepdims=True))
    alpha = jnp.exp(m_old - m_new)
    p = jnp.exp(s - m_new)
    l_sc[...] = alpha * l_sc[...] + jnp.sum(p, axis=-1, keepdims=True)
    pv = per_head(lambda h: _dot(head_rows(p, h).astype(BF16),
                                 vb_ref[0, pl.ds(h, tl, stride=nh), :].astype(BF16)))
    acc_sc[...] = alpha * acc_sc[...] + pv
    m_sc[...] = m_new

    if update:
        keep = tl * nh - shift
        ko_ref[0, :keep] = kb_ref[0, shift:]
        vo_ref[0, :keep] = vb_ref[0, shift:]

        @pl.when(lt < nlt - 1)
        def _():
            ko_ref[0, keep:] = kx_ref[0]
            vo_ref[0, keep:] = vx_ref[0]

        @pl.when(lt == nlt - 1)
        def _():
            ko_ref[0, keep:] = kn2_ref[0]
            vo_ref[0, keep:] = vn2_ref[0]

    @pl.when(lt == nlt - 1)
    def _():
        l = l_sc[...]
        out = acc_sc[...] / l
        lse = m_sc[...] + jnp.log(l)
        for h in range(nh):
            o_ref[0, :, lanes(h)] = head_rows(out, h)
            lse_ref[0, :, lanes(h)] = jnp.broadcast_to(head_rows(lse, h), (t_new, ATT_HEAD_DIM))


def _new_rows(proj3, batch0, col):
    db = proj3.shape[0] - batch0
    c0 = col * ATT_OUT_WIDTH
    return proj3[batch0:, :, c0:c0 + ATT_OUT_WIDTH].reshape(db, proj3.shape[1] * ATT_SLOTS, ATT_HEAD_DIM)


def _attn_sample(proj3, batch0, group, dilation, k_buf, v_buf, *, update):
    db, rows_total, _ = k_buf.shape
    nh = ATT_SLOTS
    length = rows_total // nh
    t_new = proj3.shape[1]
    ngroups = len(ATT_GROUPS)
    tl = _tile(length, 1024 if update else 2048, SUBLANES)
    nlt = length // tl
    step = tl // t_new
    last = length // t_new - 1
    shift = t_new * nh

    def new_spec(col):
        return pl.BlockSpec((1, t_new, ATT_OUT_WIDTH), lambda b, l: (batch0 + b, 0, col))

    new2_spec = pl.BlockSpec((1, shift, ATT_HEAD_DIM), lambda b, l: (b, 0, 0))
    buf_spec = pl.BlockSpec((1, tl * nh, ATT_HEAD_DIM), lambda b, l: (b, l, 0))
    next_spec = pl.BlockSpec((1, shift, ATT_HEAD_DIM),
                             lambda b, l: (b, jnp.minimum((l + 1) * step, last), 0))
    small = pl.BlockSpec((1, t_new, ATT_OUT_WIDTH), lambda b, l: (b, 0, 0))
    small_shape = jax.ShapeDtypeStruct((db, t_new, ATT_OUT_WIDTH), F32)
    buf_shape = jax.ShapeDtypeStruct(k_buf.shape, F32)
    rows = nh * t_new
    ins = [proj3, proj3, proj3, k_buf, v_buf]
    in_specs = [new_spec(group), new_spec(ngroups + group), new_spec(2 * ngroups + group),
                buf_spec, buf_spec]
    out_specs, out_shape = [small, small], [small_shape, small_shape]
    if update:
        ins += [_new_rows(proj3, batch0, ngroups + group), _new_rows(proj3, batch0, 2 * ngroups + group),
                k_buf, v_buf]
        in_specs += [new2_spec, new2_spec, next_spec, next_spec]
        out_specs += [buf_spec, buf_spec]
        out_shape += [buf_shape, buf_shape]
    return pl.pallas_call(
        functools.partial(_attn_sample_kernel, dilation=dilation, t_new=t_new, update=update),
        grid=(db, nlt),
        in_specs=in_specs,
        out_specs=out_specs,
        out_shape=out_shape,
        scratch_shapes=[pltpu.VMEM((rows, 1), F32), pltpu.VMEM((rows, 1), F32),
                        pltpu.VMEM((rows, ATT_HEAD_DIM), F32)],
        compiler_params=_params("parallel", "arbitrary"),
        name=f"attn_sample_d{dilation}",
    )(*ins)


def _attn_sample_sparse_kernel(q_ref, kn_ref, vn_ref, kb_ref, vb_ref, o_ref, lse_ref, *, dilation, t_new):
    nh = ATT_SLOTS
    na = kb_ref.shape[1]
    rows = nh * t_new
    nkeys = na * t_new
    dmask = dilation - 1

    def lanes(h):
        return slice(h * ATT_HEAD_DIM, (h + 1) * ATT_HEAD_DIM)

    def per_head(fn):
        return jnp.concatenate([fn(h) for h in range(nh)], axis=0)

    def head_rows(x, h):
        return x[h * t_new:(h + 1) * t_new]

    def head_keys(ref, h):
        return ref[0, :, pl.ds(h, t_new, stride=nh), :].reshape(nkeys, ATT_HEAD_DIM).astype(BF16)

    q16 = [q_ref[0, :, lanes(h)].astype(BF16) for h in range(nh)]
    sn = per_head(lambda h: _dot_nt(q16[h], kn_ref[0, :, lanes(h)].astype(BF16))) * ATT_SCALE
    t = lax.broadcasted_iota(jnp.int32, (rows, t_new), 0) % t_new
    i = lax.broadcasted_iota(jnp.int32, (rows, t_new), 1)
    sn = jnp.where((i <= t) & (((t - i) & dmask) == 0), sn, NEG_BIG)
    sb = per_head(lambda h: _dot_nt(q16[h], head_keys(kb_ref, h))) * ATT_SCALE
    t = lax.broadcasted_iota(jnp.int32, (rows, nkeys), 0) % t_new
    c = lax.broadcasted_iota(jnp.int32, (rows, nkeys), 1) % t_new
    sb = jnp.where(c == t, sb, NEG_BIG)
    m = jnp.maximum(jnp.max(sn, axis=-1, keepdims=True), jnp.max(sb, axis=-1, keepdims=True))
    pn = jnp.exp(sn - m)
    pb = jnp.exp(sb - m)
    l = jnp.sum(pn, axis=-1, keepdims=True) + jnp.sum(pb, axis=-1, keepdims=True)
    acc = per_head(lambda h: _dot(head_rows(pn, h).astype(BF16), vn_ref[0, :, lanes(h)].astype(BF16))
                   + _dot(head_rows(pb, h).astype(BF16), head_keys(vb_ref, h)))
    out = acc / l
    lse = m + jnp.log(l)
    for h in range(nh):
        o_ref[0, :, lanes(h)] = head_rows(out, h)
        lse_ref[0, :, lanes(h)] = jnp.broadcast_to(head_rows(lse, h), (t_new, ATT_HEAD_DIM))


def _attn_sample_sparse(proj3, batch0, group, dilation, k_buf, v_buf):
    db, rows_total, _ = k_buf.shape
    nh = ATT_SLOTS
    t_new = proj3.shape[1]
    ngroups = len(ATT_GROUPS)
    na = rows_total // (dilation * nh)
    view = (db, na, dilation * nh, ATT_HEAD_DIM)

    def new_spec(col):
        return pl.BlockSpec((1, t_new, ATT_OUT_WIDTH), lambda b: (batch0 + b, 0, col))

    buf_spec = pl.BlockSpec((1, na, t_new * nh, ATT_HEAD_DIM), lambda b: (b, 0, 0, 0))
    small = pl.BlockSpec((1, t_new, ATT_OUT_WIDTH), lambda b: (b, 0, 0))
    small_shape = jax.ShapeDtypeStruct((db, t_new, ATT_OUT_WIDTH), F32)
    return pl.pallas_call(
        functools.partial(_attn_sample_sparse_kernel, dilation=dilation, t_new=t_new),
        grid=(db,),
        in_specs=[new_spec(group), new_spec(ngroups + group), new_spec(2 * ngroups + group),
                  buf_spec, buf_spec],
        out_specs=[small, small],
        out_shape=[small_shape, small_shape],
        compiler_params=_params("parallel"),
        name=f"attn_sample_d{dilation}",
    )(proj3, proj3, proj3, k_buf.reshape(view), v_buf.reshape(view))


def _combine_kernel(*refs, np_tiles):
    out_ref = refs[12]

    def mix(o0, o1, o2, l0, l1, l2):
        a, b, c = l0[...], l1[...], l2[...]
        m = jnp.maximum(jnp.maximum(a, b), c)
        ea, eb, ec = jnp.exp(a - m), jnp.exp(b - m), jnp.exp(c - m)
        num = ea * o0[...] + eb * o1[...] + ec * o2[...]
        out_ref[...] = (num / (ea + eb + ec)).astype(out_ref.dtype)

    i = pl.program_id(0)
    pl.when(i < np_tiles)(lambda: mix(*refs[0:6]))
    pl.when(i >= np_tiles)(lambda: mix(*refs[6:12]))


def _combine(outs_p, lses_p, outs_s, lses_s):
    n_prompt, w = outs_p[0].shape
    n_sample = outs_s[0].shape[0]
    tm, np_tiles, prompt, sample, stacked, _ = _split_rows(n_prompt, n_sample, w, 256)
    m = n_prompt + n_sample
    return pl.pallas_call(
        functools.partial(_combine_kernel, np_tiles=np_tiles),
        grid=(m // tm,),
        in_specs=[prompt] * 6 + [sample] * 6,
        out_specs=stacked,
        out_shape=jax.ShapeDtypeStruct((m, w), BF16),
        compiler_params=_params("arbitrary"),
        name="attn_combine",
    )(*outs_p, *lses_p, *outs_s, *lses_s)


def _cumsum_rows(x):
    c = x.shape[0]
    r = lax.broadcasted_iota(jnp.int32, (c, c), 0)
    s = lax.broadcasted_iota(jnp.int32, (c, c), 1)
    tri = jnp.where(r >= s, 1.0, 0.0).astype(BF16)
    hi = x.astype(BF16)
    rem = x - hi.astype(F32)
    mid = rem.astype(BF16)
    lo = (rem - mid.astype(F32)).astype(BF16)
    return _dot(tri, hi) + _dot(tri, mid) + _dot(tri, lo)


def _col_bcast(w):
    hi = w.astype(BF16).astype(F32)
    mid = (w - hi).astype(BF16).astype(F32)
    lo = (w - hi - mid).astype(BF16).astype(F32)
    terms = jnp.concatenate([hi, mid, lo, jnp.zeros((SUBLANES - 3, LANES), F32)], axis=0).astype(BF16)
    ones = jnp.where(lax.broadcasted_iota(jnp.int32, (SUBLANES, LANES), 0) < 3, 1.0, 0.0).astype(BF16)
    return _dot_tn(terms, ones)


def _lower_bound(lb_ref):
    a = lb_ref[...]
    e = jnp.exp(a - jnp.max(a, axis=0, keepdims=True))
    return e[0:1] / jnp.sum(e, axis=0, keepdims=True)


def _hgrn_chunk(qg, fg, ig, gg, lb, gain, st):
    c = qg.shape[0]
    q = qg * _sigmoid(qg) * HG_SCALE
    forget = lb + (1.0 - lb) * _sigmoid(fg)
    k = 1.0 - forget
    v = ig
    b = _cumsum_rows(jnp.log(forget))
    v16 = v.astype(BF16)

    nsub = c // HG_SUB
    b3 = b.reshape(nsub, HG_SUB, LANES)
    q3 = q.reshape(nsub, HG_SUB, LANES)
    k3 = k.reshape(nsub, HG_SUB, LANES)
    pos = lax.broadcasted_iota(jnp.int32, (nsub, HG_SUB, LANES), 1)
    row = lax.broadcasted_iota(jnp.int32, (c, c), 0)
    col = lax.broadcasted_iota(jnp.int32, (c, c), 1)
    scores = jnp.zeros((c, c), F32)
    for s in range(HG_SUB):
        diff = jnp.where(pos >= s, b3 - b3[:, s:s + 1, :], NEG_BIG)
        w = jnp.sum(q3 * k3[:, s:s + 1, :] * jnp.exp(diff), axis=-1, keepdims=True)
        w = jnp.broadcast_to(w.reshape(c, 1), (c, c))
        scores = jnp.where(col == (row // HG_SUB) * HG_SUB + s, w, scores)

    width = HG_SUB
    while width < c:
        pair = 2 * width
        bm = jnp.concatenate(
            [jnp.broadcast_to(b[p * pair + width - 1:p * pair + width], (pair, LANES))
             for p in range(c // pair)], axis=0)
        e = jnp.exp(-jnp.abs(b - bm))
        right = (lax.broadcasted_iota(jnp.int32, (c, LANES), 0) // width) % 2 == 1
        ql = jnp.where(right, q * e, 0.0).astype(BF16)
        kl = jnp.where(right, 0.0, k * e).astype(BF16)
        a = _dot_nt(ql, kl)
        scores = jnp.where((row // pair == col // pair) & (row // width != col // width) & (row > col),
                           a, scores)
        width = pair

    o = _dot(scores.astype(BF16), v16) + _dot_nt((q * jnp.exp(b)).astype(BF16), st.astype(BF16))
    b_last = b[c - 1:c]
    st_new = st * jnp.exp(b_last) + _dot_tn(v16, (k * jnp.exp(b_last - b)).astype(BF16))
    out = _rms(o, gain) * (gg * _sigmoid(gg))
    return out, st_new


def _hgrn_prompt_kernel(q_ref, f_ref, i_ref, g_ref, lb_ref, gain_ref, o_ref, s_ref, st_sc):
    tb = pl.program_id(1)

    @pl.when(tb == 0)
    def _():
        st_sc[...] = jnp.zeros_like(st_sc)

    nheads = st_sc.shape[0]
    gain = gain_ref[...]
    lbs = [_lower_bound(lb_ref.at[:, h * LANES:(h + 1) * LANES]) for h in range(nheads)]
    sts = [st_sc[h] for h in range(nheads)]
    for ci in range(q_ref.shape[0] // HG_CHUNK):
        rs = slice(ci * HG_CHUNK, (ci + 1) * HG_CHUNK)
        for h in range(nheads):
            sl = slice(h * LANES, (h + 1) * LANES)
            out, sts[h] = _hgrn_chunk(q_ref[rs, sl], f_ref[rs, sl], i_ref[rs, sl], g_ref[rs, sl],
                                      lbs[h], gain, sts[h])
            o_ref[rs, sl] = out.astype(o_ref.dtype)
    for h in range(nheads):
        st_sc[h] = sts[h]

    @pl.when(tb == pl.num_programs(1) - 1)
    def _():
        for h in range(nheads):
            s_ref[h] = sts[h].T


def _hgrn_prompt(proj, seq, hg_lower_bound, hg_norm, hg_width):
    heads = hg_width // HG_EXPAND
    hw = HG_EXPAND
    assert 3 * ATT_WIDTH % hw == 0
    base = 3 * ATT_WIDTH // hw
    per = hg_width // hw
    hpb = hw // HG_EXPAND
    tb = _tile(seq, 512, HG_CHUNK)

    def col(which):
        return pl.BlockSpec((tb, hw), lambda g, t: (t, base + which * per + g))

    depth1 = hg_lower_bound.shape[0]
    return pl.pallas_call(
        _hgrn_prompt_kernel,
        grid=(per, seq // tb),
        in_specs=[col(0), col(1), col(2), col(3),
                  pl.BlockSpec((depth1, hw), lambda g, t: (0, g)),
                  pl.BlockSpec((1, LANES), lambda g, t: (0, 0))],
        out_specs=[pl.BlockSpec((tb, hw), lambda g, t: (t, g)),
                   pl.BlockSpec((hpb, HG_EXPAND, HG_HEAD_V), lambda g, t: (g, 0, 0))],
        out_shape=[jax.ShapeDtypeStruct((seq, hg_width), BF16),
                   jax.ShapeDtypeStruct((heads, HG_EXPAND, HG_HEAD_V), F32)],
        scratch_shapes=[pltpu.VMEM((hpb, HG_EXPAND, HG_HEAD_V), F32)],
        compiler_params=_params("parallel", "arbitrary"),
        name="hgrn_prompt",
    )(proj, proj, proj, proj, hg_lower_bound, hg_norm.reshape(1, LANES))


def _hgrn_sample_kernel(q_ref, f_ref, i_ref, g_ref, lb_ref, gain_ref, s_ref, o_ref, so_ref):
    nheads = s_ref.shape[1]
    c = q_ref.shape[1]
    qg, fg, v, gg = q_ref[0], f_ref[0], i_ref[0], g_ref[0]
    lb = _lower_bound(lb_ref)
    gain = gain_ref[...]
    q = qg * _sigmoid(qg) * HG_SCALE
    forget = lb + (1.0 - lb) * _sigmoid(fg)
    k = 1.0 - forget
    b = _cumsum_rows(jnp.log(forget))
    b_last = b[c - 1:c]
    qdec = (q * jnp.exp(b)).astype(BF16)
    kdec = (k * jnp.exp(b_last - b)).astype(BF16)
    decay = jnp.exp(b_last)
    v16 = v.astype(BF16)
    t = lax.broadcasted_iota(jnp.int32, b.shape, 0)
    pair = [q * k[s:s + 1] * jnp.exp(jnp.where(t >= s, b - b[s:s + 1], NEG_BIG)) for s in range(c)]
    outs = []
    for h in range(nheads):
        sl = slice(h * LANES, (h + 1) * LANES)
        st = s_ref[0, h]
        o = _dot(qdec[:, sl], st.astype(BF16))
        for s in range(c):
            o = o + jnp.sum(pair[s][:, sl], axis=-1, keepdims=True) * v[s:s + 1, sl]
        so_ref[0, h] = st * _col_bcast(decay[:, sl]) + _dot_tn(kdec[:, sl], v16[:, sl])
        outs.append(_rms(o, gain))
    o_ref[0] = (jnp.concatenate(outs, axis=1) * (gg * _sigmoid(gg))).astype(o_ref.dtype)


def _hgrn_sample(proj3, batch0, state, hg_lower_bound, hg_norm):
    db, heads = state.shape[:2]
    t_new = proj3.shape[1]
    hg_width = heads * HG_EXPAND
    cw = _tile(hg_width, ATT_OUT_WIDTH, LANES)
    assert 3 * ATT_WIDTH % cw == 0
    base = 3 * ATT_WIDTH // cw
    per = hg_width // cw
    hb = cw // HG_EXPAND

    def col(which):
        return pl.BlockSpec((1, t_new, cw), lambda b, c: (batch0 + b, 0, base + which * per + c))

    depth1 = hg_lower_bound.shape[0]
    st_spec = pl.BlockSpec((1, hb, HG_EXPAND, HG_HEAD_V), lambda b, c: (b, c, 0, 0))
    return pl.pallas_call(
        _hgrn_sample_kernel,
        grid=(db, per),
        in_specs=[col(0), col(1), col(2), col(3),
                  pl.BlockSpec((depth1, cw), lambda b, c: (0, c)),
                  pl.BlockSpec((1, LANES), lambda b, c: (0, 0)),
                  st_spec],
        out_specs=[pl.BlockSpec((1, t_new, cw), lambda b, c: (b, 0, c)), st_spec],
        out_shape=[jax.ShapeDtypeStruct((db, t_new, hg_width), BF16),
                   jax.ShapeDtypeStruct(state.shape, F32)],
        compiler_params=_params("parallel", "parallel"),
        name="hgrn_sample",
    )(proj3, proj3, proj3, proj3, hg_lower_bound, hg_norm.reshape(1, LANES), state)


def kernel(x_prompt, x_sample, cache_k_w128, cache_v_w128, cache_k_w512, cache_v_w512,
           cache_k_w2048, cache_v_w2048, state_hgrn, hg_lower_bound, w_in, w_gate, b_gate,
           w_proj_att, w_proj_hg, w_out, hg_norm, norm_mix_pre, norm_mix_post,
           norm_ffn_pre, norm_ffn_post, w_up, w_down):
    assert w_in.shape[0] == 1, "single-layer trunk"
    batch, seq, d_model = x_prompt.shape
    db, t_new, _ = x_sample.shape
    assert batch == 1
    hg_width = w_proj_hg.shape[1]
    n_prompt = batch * seq
    n_sample = db * t_new
    caches = (cache_k_w128, cache_v_w128, cache_k_w512, cache_v_w512, cache_k_w2048, cache_v_w2048)

    xp = x_prompt.reshape(n_prompt, d_model)
    xs = x_sample.reshape(n_sample, d_model)
    h = _norm_cast(xp, xs, norm_mix_pre[0])
    proj = _matmul(h, w_in[0].astype(BF16), name="in_proj")
    gates = _matmul(h, w_gate[0].astype(BF16), bias=b_gate[0], name="gate_proj")
    in_width = proj.shape[1]
    proj3 = proj.reshape((n_prompt + n_sample) // t_new, t_new, in_width)
    batch0 = n_prompt // t_new

    ngroups = len(ATT_GROUPS)
    mlp_steps = ((n_prompt + n_sample) // _tile(n_prompt + n_sample, 1024, SUBLANES)
                 * (w_up.shape[2] // _tile(w_up.shape[2], 1024, LANES)))
    pending = [None, None]
    outs_p, lses_p, outs_s, lses_s, new_kv_p, new_kv_s = [], [], [], [], [], []
    for g, (window, dilation) in enumerate(ATT_GROUPS):
        o, lse = _attn_prompt(proj, seq, g, dilation)
        outs_p.append(o)
        lses_p.append(lse)
        keep = min(window, seq)
        for part in (1, 2):
            c0 = part * ATT_WIDTH + g * ATT_OUT_WIDTH
            rows = proj[n_prompt - keep:n_prompt, c0:c0 + ATT_OUT_WIDTH]
            new_kv_p.append(rows.reshape(1, batch, keep, ATT_SLOTS, ATT_HEAD_DIM))
        k_buf, v_buf = caches[2 * g][0], caches[2 * g + 1][0]
        length = k_buf.shape[1]
        assert length == window and length == dilation * ATT_BLOCK
        k_buf = k_buf.reshape(db, length * ATT_SLOTS, ATT_HEAD_DIM)
        v_buf = v_buf.reshape(db, length * ATT_SLOTS, ATT_HEAD_DIM)
        deferred = db + 1 <= mlp_steps and g == ngroups - 1
        if deferred and dilation >= t_new:
            res = _attn_sample_sparse(proj3, batch0, g, dilation, k_buf, v_buf)
        else:
            res = _attn_sample(proj3, batch0, g, dilation, k_buf, v_buf, update=not deferred)
        outs_s.append(res[0].reshape(n_sample, ATT_OUT_WIDTH))
        lses_s.append(res[1].reshape(n_sample, ATT_OUT_WIDTH))
        if deferred:
            pending = [(k_buf, _new_rows(proj3, batch0, ngroups + g)),
                       (v_buf, _new_rows(proj3, batch0, 2 * ngroups + g))]
        else:
            new_kv_s += list(res[2:])
    att = _combine(outs_p, lses_p, outs_s, lses_s)

    hg_p, state_p = _hgrn_prompt(proj, seq, hg_lower_bound, hg_norm[0], hg_width)
    hg_s, state_s = _hgrn_sample(proj3, batch0, state_hgrn[0], hg_lower_bound, hg_norm[0])
    hg = jnp.concatenate([hg_p, hg_s.reshape(n_sample, hg_width)], axis=0)

    merged = _merge_proj(att, hg, w_proj_att[0].astype(BF16), w_proj_hg[0].astype(BF16), gates)
    mixed = _matmul(merged, w_out[0].astype(BF16), name="out_proj")
    x1, h2 = _resid_norm(xp, xs, mixed, norm_mix_post[0], norm_ffn_pre[0])

    u = _matmul(h2, w_up[0].astype(BF16), out_dtype=BF16, act="relu2", update=pending[0], name="ffn_up")
    if pending[0] is not None:
        u, k_last = u
    z = _matmul_ksplit(u, w_down[0].astype(BF16), tn=1024, tk=2048, update=pending[1], name="ffn_down")
    if pending[1] is not None:
        z, v_last = z
        new_kv_s += [k_last, v_last]
    y_prompt, y_sample = _resid_final(x1, z, norm_ffn_post[0], n_prompt)
    y_prompt = y_prompt.reshape(batch, seq, d_model)
    y_sample = y_sample.reshape(db, t_new, d_model)
    heads = hg_width // HG_EXPAND
    new_kv_s = [c.reshape(src.shape) for c, src in zip(new_kv_s, caches)]
    return (y_prompt, y_sample, *new_kv_p,
            state_p.reshape(1, batch, heads, HG_EXPAND, HG_HEAD_V),
            *new_kv_s, state_s[None])
```
